```python
import math
import jax, jax.numpy as jnp
from jax import lax
import numpy as np

D_MODEL = 1024
BATCH = 2
SEQ = 8192
DEPTH = 2

GRID_W = 64
Q_BLOCK = 128
EPS = 1e-6
ROPE_THETA = 10000.0

A_HEADS = 8
A_KV_HEADS = 2
A_HEAD_DIM = 64
A_WIDTH = A_HEADS * A_HEAD_DIM
A_KV_WIDTH = A_KV_HEADS * A_HEAD_DIM

B_WIDTH = D_MODEL // 2
HY_ORDER = 2
HY_SHORT = 3
HY_EMB_BANDS = 16
HY_EMB_DIM = 1 + 2 * HY_EMB_BANDS
HY_FILTER_HIDDEN = 64
HY_DECAY_TARGET = 1e-2
HY_FAST_DECAY = 0.3
HY_SLOW_DECAY = 1.5

EVEN_IN = A_WIDTH + 2 * A_KV_WIDTH + A_WIDTH + (HY_ORDER + 1) * B_WIDTH + B_WIDTH

C_HEADS = 16
C_NOPE = 64
C_ROPE = 32
C_V = 64
C_Q_RANK = 384
C_KV_RANK = 256
C_WIDTH = C_HEADS * C_V
ODD_IN = C_Q_RANK + C_KV_RANK + C_ROPE + C_WIDTH

kernel_name = 'hybrid_gqa_hyena_mla_encoder'

F32 = jnp.float32


def rmsnorm(x, g):
    xf = x.astype(F32)
    y = xf * lax.rsqrt(jnp.mean(xf * xf, axis=-1, keepdims=True) + EPS)
    return (y * g.astype(F32)).astype(x.dtype)


def split_cols(p, widths):
    out, start = [], 0
    for wd in widths:
        out.append(p[..., start:start + wd])
        start += wd
    return out


def axial_rope_angles(L, dim):
    rows = L // GRID_W
    row = jnp.repeat(jnp.arange(rows), GRID_W).astype(F32)
    col = jnp.tile(jnp.arange(GRID_W), rows).astype(F32)
    n = dim // 4
    inv = ROPE_THETA ** (-jnp.arange(n, dtype=F32) / n)
    ang = jnp.concatenate([row[:, None] * inv, col[:, None] * inv], axis=-1)
    return jnp.cos(ang), jnp.sin(ang)


def apply_rope(x, cos, sin):
    xf = x.astype(F32)
    half = x.shape[-1] // 2
    x1, x2 = xf[..., :half], xf[..., half:]
    return jnp.concatenate([x1 * cos - x2 * sin, x1 * sin + x2 * cos], axis=-1).astype(x.dtype)


def blocked_attention(q, k, v, scale):
    b, hk, g, L, dq = q.shape
    nb = L // Q_BLOCK
    qb = jnp.moveaxis(q.reshape(b, hk, g, nb, Q_BLOCK, dq), 3, 0)

    def one_block(qblk):
        s = jnp.einsum('bhgqd,bhkd->bhgqk', qblk, k, preferred_element_type=F32) * scale
        p = jax.nn.softmax(s, axis=-1)
        return jnp.einsum('bhgqk,bhkd->bhgqd', p.astype(v.dtype), v)

    o = lax.map(one_block, qb)
    return jnp.moveaxis(o, 0, 3).reshape(b, hk, g, L, v.shape[-1])


def short_conv(u, w, bias):
    y = lax.conv_general_dilated(u, w[:, None, :].astype(u.dtype), window_strides=(1,),
                                 padding=((HY_SHORT // 2, HY_SHORT // 2),),
                                 dimension_numbers=('NWC', 'WIO', 'NWC'),
                                 feature_group_count=u.shape[-1])
    return y + bias.astype(u.dtype)


def hyena_filters(L, w1, b1, f1, w2, b2, f2, w3):
    t = jnp.linspace(0.0, 1.0, L, dtype=F32)[:, None]
    w = 2.0 * math.pi * jnp.arange(L, dtype=F32)[:, None] / L
    bands = jnp.linspace(1e-4, HY_EMB_BANDS - 1, HY_EMB_BANDS, dtype=F32)
    z = jnp.concatenate([t, jnp.cos(bands * w), -jnp.sin(bands * w)], axis=-1)
    h = jnp.sin(f1.astype(F32) * (z @ w1.astype(F32) + b1.astype(F32)))
    h = jnp.sin(f2.astype(F32) * (h @ w2.astype(F32) + b2.astype(F32)))
    h = h @ w3.astype(F32)
    min_decay = math.log(HY_DECAY_TARGET) / HY_SLOW_DECAY
    max_decay = math.log(HY_DECAY_TARGET) / HY_FAST_DECAY
    deltas = jnp.abs(jnp.linspace(min_decay, max_decay, B_WIDTH, dtype=F32))
    decay = jnp.exp(-t * deltas)
    h_fwd = h[:, :B_WIDTH] * decay
    h_bwd = h[:, B_WIDTH:] * decay
    l1 = jnp.sum(jnp.abs(h_fwd), axis=0, keepdims=True) + jnp.sum(jnp.abs(h_bwd[1:]), axis=0, keepdims=True)
    return h_fwd / l1, h_bwd / l1


def bidir_fftconv(u, h_fwd, h_bwd):
    L, C = h_fwd.shape
    n = 2 * L
    k = jnp.concatenate([h_fwd, jnp.zeros((1, C), F32), h_bwd[1:][::-1]], axis=0)
    U = jnp.fft.rfft(u, n=n, axis=1)
    K = jnp.fft.rfft(k, n=n, axis=0)
    return jnp.fft.irfft(U * K[None], n=n, axis=1)[:, :L]


def hyena_mixer(p, conv_w, conv_b, fw1, fb1, ff1, fw2, fb2, ff2, fw3, hy_d):
    L = p.shape[1]
    u = short_conv(p, conv_w, conv_b)
    x0, x1, v = split_cols(u, [B_WIDTH, B_WIDTH, B_WIDTH])
    z = (v * x1).astype(F32)
    hf, hb = hyena_filters(L, fw1, fb1, ff1, fw2, fb2, ff2, fw3)
    y = bidir_fftconv(z, hf, hb) + z * hy_d.astype(F32)
    return (y * x0.astype(F32)).astype(p.dtype)


def even_mixer(h, w_in, q_norm, k_norm, conv_w, conv_b, fw1, fb1, ff1, fw2, fb2, ff2, fw3, hy_d, w_out):
    b, L, _ = h.shape
    p = jnp.einsum('bld,de->ble', h, w_in)
    q, k, v, ga, hy, gb = split_cols(p, [A_WIDTH, A_KV_WIDTH, A_KV_WIDTH, A_WIDTH,
                                         (HY_ORDER + 1) * B_WIDTH, B_WIDTH])
    cos, sin = axial_rope_angles(L, A_HEAD_DIM)
    cos4, sin4 = cos[None, :, None], sin[None, :, None]
    q = apply_rope(rmsnorm(q.reshape(b, L, A_HEADS, A_HEAD_DIM), q_norm), cos4, sin4)
    k = apply_rope(rmsnorm(k.reshape(b, L, A_KV_HEADS, A_HEAD_DIM), k_norm), cos4, sin4)
    v = v.reshape(b, L, A_KV_HEADS, A_HEAD_DIM)
    grp = A_HEADS // A_KV_HEADS
    qg = q.reshape(b, L, A_KV_HEADS, grp, A_HEAD_DIM).transpose(0, 2, 3, 1, 4)
    o = blocked_attention(qg, k.transpose(0, 2, 1, 3), v.transpose(0, 2, 1, 3), A_HEAD_DIM ** -0.5)
    ya = o.transpose(0, 3, 1, 2, 4).reshape(b, L, A_WIDTH) * jax.nn.silu(ga)
    yb = hyena_mixer(hy, conv_w, conv_b, fw1, fb1, ff1, fw2, fb2, ff2, fw3, hy_d) * jax.nn.silu(gb)
    return jnp.einsum('ble,ed->bld', jnp.concatenate([ya, yb], axis=-1), w_out)


def odd_mixer(h, w_in, q_a_norm, w_qb, kv_a_norm, w_kvb, w_out):
    b, L, _ = h.shape
    p = jnp.einsum('bld,de->ble', h, w_in)
    cq, ckv, kr, gc = split_cols(p, [C_Q_RANK, C_KV_RANK, C_ROPE, C_WIDTH])
    cos, sin = axial_rope_angles(L, C_ROPE)
    q = jnp.einsum('blr,re->ble', rmsnorm(cq, q_a_norm), w_qb).reshape(b, L, C_HEADS, C_NOPE + C_ROPE)
    q = jnp.concatenate([q[..., :C_NOPE],
                         apply_rope(q[..., C_NOPE:], cos[None, :, None], sin[None, :, None])], axis=-1)
    kv = jnp.einsum('blr,re->ble', rmsnorm(ckv, kv_a_norm), w_kvb).reshape(b, L, C_HEADS, C_NOPE + C_V)
    kr = apply_rope(kr, cos[None], sin[None])
    k = jnp.concatenate([kv[..., :C_NOPE],
                         jnp.broadcast_to(kr[:, :, None, :], (b, L, C_HEADS, C_ROPE))], axis=-1)
    v = kv[..., C_NOPE:]
    o = blocked_attention(q.transpose(0, 2, 1, 3)[:, :, None], k.transpose(0, 2, 1, 3),
                          v.transpose(0, 2, 1, 3), (C_NOPE + C_ROPE) ** -0.5)
    yc = o[:, :, 0].transpose(0, 2, 1, 3).reshape(b, L, C_WIDTH) * jax.nn.silu(gc)
    return jnp.einsum('ble,ed->bld', yc, w_out)


def setup_inputs(seed: int = 0) -> dict:
    key = jax.random.key(seed)
    ks = iter(jax.random.split(key, 40))
    ne = (DEPTH + 1) // 2
    no = DEPTH // 2

    def w(shape, fan_in):
        return jax.random.normal(next(ks), shape, F32) * (fan_in ** -0.5)

    def gain(shape, s=0.02):
        return 1.0 + s * jax.random.normal(next(ks), shape, F32)

    def small(shape, s=0.02):
        return s * jax.random.normal(next(ks), shape, F32)

    C = B_WIDTH
    return {
        'x': jax.random.normal(next(ks), (BATCH, SEQ, D_MODEL), F32),
        'e_norm': gain((ne, D_MODEL)),
        'e_w_in': w((ne, D_MODEL, EVEN_IN), D_MODEL),
        'e_q_norm': gain((ne, A_HEAD_DIM)),
        'e_k_norm': gain((ne, A_HEAD_DIM)),
        'e_conv_w': w((ne, HY_SHORT, (HY_ORDER + 1) * C), HY_SHORT),
        'e_conv_b': small((ne, (HY_ORDER + 1) * C)),
        'e_filt_w1': w((ne, HY_EMB_DIM, HY_FILTER_HIDDEN), HY_EMB_DIM),
        'e_filt_b1': small((ne, HY_FILTER_HIDDEN)),
        'e_filt_f1': gain((ne, HY_FILTER_HIDDEN), 0.1),
        'e_filt_w2': w((ne, HY_FILTER_HIDDEN, HY_FILTER_HIDDEN), HY_FILTER_HIDDEN),
        'e_filt_b2': small((ne, HY_FILTER_HIDDEN)),
        'e_filt_f2': gain((ne, HY_FILTER_HIDDEN), 0.1),
        'e_filt_w3': w((ne, HY_FILTER_HIDDEN, 2 * C), HY_FILTER_HIDDEN),
        'e_hy_d': small((ne, C), 0.1),
        'e_w_out': w((ne, A_WIDTH + B_WIDTH, D_MODEL), A_WIDTH + B_WIDTH),
        'o_norm': gain((no, D_MODEL)),
        'o_w_in': w((no, D_MODEL, ODD_IN), D_MODEL),
        'o_q_a_norm': gain((no, C_Q_RANK)),
        'o_w_qb': w((no, C_Q_RANK, C_HEADS * (C_NOPE + C_ROPE)), C_Q_RANK),
        'o_kv_a_norm': gain((no, C_KV_RANK)),
        'o_w_kvb': w((no, C_KV_RANK, C_HEADS * (C_NOPE + C_V)), C_KV_RANK),
        'o_w_out': w((no, C_WIDTH, D_MODEL), C_WIDTH),
        'final_norm': gain((D_MODEL,)),
    }


def reference(x, e_norm, e_w_in, e_q_norm, e_k_norm, e_conv_w, e_conv_b, e_filt_w1, e_filt_b1,
              e_filt_f1, e_filt_w2, e_filt_b2, e_filt_f2, e_filt_w3, e_hy_d, e_w_out,
              o_norm, o_w_in, o_q_a_norm, o_w_qb, o_kv_a_norm, o_w_kvb, o_w_out, final_norm):
    for i in range(DEPTH):
        j = i // 2
        if i % 2 == 0:
            x = x + even_mixer(rmsnorm(x, e_norm[j]), e_w_in[j], e_q_norm[j], e_k_norm[j],
                               e_conv_w[j], e_conv_b[j], e_filt_w1[j], e_filt_b1[j], e_filt_f1[j],
                               e_filt_w2[j], e_filt_b2[j], e_filt_f2[j], e_filt_w3[j], e_hy_d[j],
                               e_w_out[j])
        else:
            x = x + odd_mixer(rmsnorm(x, o_norm[j]), o_w_in[j], o_q_a_norm[j], o_w_qb[j],
                              o_kv_a_norm[j], o_w_kvb[j], o_w_out[j])
    return rmsnorm(x, final_norm)
```

```python
import functools
import math

import numpy as np
import jax
import jax.numpy as jnp
from jax import lax
from jax.experimental import pallas as pl
from jax.experimental.pallas import tpu as pltpu

F32 = jnp.float32
BF16 = jnp.bfloat16

D_MODEL = 1024
GRID_W = 64
EPS = 1e-6
ROPE_THETA = 10000.0

A_HEADS, A_KV_HEADS, A_HEAD_DIM = 8, 2, 64
A_WIDTH = A_HEADS * A_HEAD_DIM
HY_C = D_MODEL // 2
HY_EMB_BANDS = 16
HY_EMB_DIM = 1 + 2 * HY_EMB_BANDS
HY_HIDDEN = 64
HY_DECAY_TARGET, HY_FAST_DECAY, HY_SLOW_DECAY = 1e-2, 0.3, 1.5
EVEN_IN = 3328

C_HEADS, C_NOPE, C_ROPE, C_V = 16, 64, 32, 64
C_Q_RANK, C_KV_RANK = 384, 256
C_WIDTH = C_HEADS * C_V

LANES = 128
DFT_N2 = 128
FEAT_W = 64
VMEM_LIMIT = 56 * 1024 * 1024


def _cparams(sem):
    return pltpu.CompilerParams(dimension_semantics=sem, vmem_limit_bytes=VMEM_LIMIT)


def _silu(x):
    return x * (1.0 / (1.0 + jnp.exp(-x)))


def _dot(a, b):
    return jnp.dot(a, b, preferred_element_type=F32)


def _split_bf16(a):
    hi = a.astype(BF16)
    return hi, (a - hi.astype(F32)).astype(BF16)


def _dot3(a, w_hi, w_lo):
    a_hi, a_lo = _split_bf16(a)
    return _dot(a_hi, w_hi) + (_dot(a_hi, w_lo) + _dot(a_lo, w_hi))


def _axial_angles(L, dim):
    rows = L // GRID_W
    row = np.repeat(np.arange(rows), GRID_W).astype(np.float64)
    col = np.tile(np.arange(GRID_W), rows).astype(np.float64)
    n = dim // 4
    inv = ROPE_THETA ** (-np.arange(n, dtype=np.float64) / n)
    return np.concatenate([row[:, None] * inv, col[:, None] * inv], axis=-1)


def _rope_tables_a(L):
    ang = _axial_angles(L, A_HEAD_DIM)
    lane = np.arange(LANES)
    j = lane % 32
    sign = np.where(lane // 64 == 0, -1.0, 1.0)
    return (np.cos(ang)[:, j].astype(np.float32), (np.sin(ang)[:, j] * sign).astype(np.float32))


def _rope_tables_c(L):
    ang = _axial_angles(L, C_ROPE)
    cs = np.zeros((L, LANES)); sn = np.zeros((L, LANES))
    cs[:, 0:32] = 1.0; cs[:, 64:96] = 1.0
    cs[:, 32:48] = np.cos(ang); cs[:, 96:112] = np.cos(ang)
    sn[:, 32:48] = -np.sin(ang); sn[:, 96:112] = np.sin(ang)
    return cs.astype(np.float32), sn.astype(np.float32)


def _even_columns():
    q0, k0, v0, rest0 = 0, 512, 640, 768
    idx, keep = [], []
    lane = np.arange(LANES)
    part, hs, j = lane // 64, (lane % 64) // 32, lane % 32
    for pair in range(A_HEADS // 2):
        idx.append(q0 + (2 * pair + hs) * 64 + 32 * part + j); keep.append(np.ones(LANES))
    for g in range(A_KV_HEADS):
        for var in range(2):
            idx.append(k0 + g * 64 + 32 * part + j); keep.append((hs == var).astype(np.float64))
    for g in range(A_KV_HEADS):
        idx.append(v0 + g * 64 + lane % 64); keep.append(np.ones(LANES))
    idx.append(np.arange(rest0, EVEN_IN)); keep.append(np.ones(EVEN_IN - rest0))
    return np.concatenate(idx).astype(np.int32), np.concatenate(keep).astype(np.float32)


def _head_gain_index():
    lane = np.arange(LANES)
    return (32 * (lane // 64) + lane % 32).astype(np.int32)


def _segment_mean_matrix():
    lane = np.arange(LANES)
    cls = (lane % 64) // 32
    return ((cls[:, None] == cls[None, :]) / float(A_HEAD_DIM)).astype(np.float32)


def _mla_head_lane_sources():
    src = -np.ones(LANES, np.int64)
    src[0:32] = np.arange(32)
    src[32:48] = 64 + np.arange(16)
    src[64:96] = 32 + np.arange(32)
    src[96:112] = 80 + np.arange(16)
    return src


def _odd_columns():
    src = _mla_head_lane_sources()
    kr = np.where(src >= 64, 640 + (src - 64), 0)
    kr_keep = (src >= 64).astype(np.float32)
    in_idx = np.concatenate([np.arange(0, 640), kr, np.arange(672, 1696)]).astype(np.int32)
    in_keep = np.concatenate([np.ones(640), kr_keep, np.ones(1024)]).astype(np.float32)
    q_idx, q_keep, k_idx, k_keep, v_idx = [], [], [], [], []
    for h in range(C_HEADS):
        q_idx.append(np.where(src >= 0, h * (C_NOPE + C_ROPE) + src, 0)); q_keep.append(src >= 0)
        nope = (src >= 0) & (src < 64)
        k_idx.append(np.where(nope, h * (C_NOPE + C_V) + src, 0)); k_keep.append(nope)
        v_idx.append(h * (C_NOPE + C_V) + C_NOPE + np.arange(C_V))
    cat = lambda xs, dt: np.concatenate(xs).astype(dt)
    return (in_idx, in_keep, cat(q_idx, np.int32), cat(q_keep, np.float32),
            cat(k_idx, np.int32), cat(k_keep, np.float32), cat(v_idx, np.int32))


def _gather_cols(w, idx, keep):
    pieces, start, n = [], 0, len(idx)
    while start < n:
        end = start + 1
        while end < n and keep[end] == keep[start] and (keep[start] == 0 or idx[end] == idx[end - 1] + 1):
            end += 1
        if keep[start]:
            pieces.append(w[:, int(idx[start]):int(idx[start]) + end - start])
        else:
            pieces.append(jnp.zeros((w.shape[0], end - start), w.dtype))
        start = end
    return jnp.concatenate(pieces, axis=1)


def _dft_constants(L):
    n1 = 2 * L // DFT_N2
    n = n1 * DFT_N2
    h = n1 // 2
    a1 = -2.0 * np.pi * np.outer(np.arange(n1), np.arange(n1)) / n1
    f1r, f1i = np.cos(a1), np.sin(a1)
    a2 = -2.0 * np.pi * np.outer(np.arange(DFT_N2), np.arange(DFT_N2)) / DFT_N2
    f2r, f2i = np.cos(a2), np.sin(a2)
    at = -2.0 * np.pi * np.outer(np.arange(n1), np.arange(DFT_N2)) / n
    fd = np.block([[f1r[:, :h], -f1i[:, :h]], [f1i[:, :h], f1r[:, :h]]])
    ff = np.concatenate([f1r, f1i], axis=0)
    fb = np.block([[f2r, -f2i], [f2i, f2r]])
    ib = np.block([[f2r, f2i], [-f2i, f2r]])
    if2 = np.block([[f1r[:h], f1i[:h]], [-f1i[:h], f1r[:h]]]) / n
    return dict(n1=n1, fd=fd, ff=ff, fb=fb, ib=ib, if2=if2,
                twr=np.cos(at)[:, :, None].astype(np.float32),
                twi=np.sin(at)[:, :, None].astype(np.float32))


def _filter_features(L):
    r = np.arange(2 * L)
    lag = np.where(r < L, r, 2 * L - r)
    lag = np.where(r == L, 0, lag)
    t = lag / float(L - 1)
    w = 2.0 * np.pi * lag / float(L)
    bands = np.linspace(1e-4, HY_EMB_BANDS - 1, HY_EMB_BANDS)
    feat = np.zeros((2 * L, FEAT_W))
    feat[:, 0] = t
    feat[:, 1:1 + HY_EMB_BANDS] = np.cos(bands[None] * w[:, None])
    feat[:, 1 + HY_EMB_BANDS:HY_EMB_DIM] = -np.sin(bands[None] * w[:, None])
    feat[:, HY_EMB_DIM] = (r != L)
    return feat.astype(np.float32)


def _decay_rates():
    lo = math.log(HY_DECAY_TARGET) / HY_SLOW_DECAY
    hi = math.log(HY_DECAY_TARGET) / HY_FAST_DECAY
    return np.abs(np.linspace(lo, hi, HY_C)).astype(np.float32)[None]


def _rms_rows(x, gain):
    return x * lax.rsqrt(jnp.mean(x * x, axis=-1, keepdims=True) + EPS) * gain


def _rope(y, cs, sn):
    return y * cs + pltpu.roll(y, 64, 1) * sn


def _even_in_kernel(x_ref, g_ref, w_ref, m_ref, gq_ref, gk_ref, cs_ref, sn_ref,
                    q_ref, kt_ref, v_ref, ga_ref, hy_ref, gb_ref, *, scale):
    h = _rms_rows(x_ref[...], g_ref[...]).astype(BF16)
    cs, sn, m = cs_ref[...], sn_ref[...], m_ref[...]

    def norm_rope(p, gain):
        ms = _dot((p * p).astype(BF16), m)
        return _rope(p * lax.rsqrt(ms + EPS) * gain, cs, sn)

    pq = _dot(h, w_ref[:, 0:512])
    pk = _dot(h, w_ref[:, 512:1024])
    for b in range(4):
        sl = slice(b * LANES, (b + 1) * LANES)
        q_ref[:, sl] = (norm_rope(pq[:, sl], gq_ref[...]) * scale).astype(BF16)
        kt_ref[sl, :] = norm_rope(pk[:, sl], gk_ref[...]).T.astype(BF16)
    v_ref[...] = _dot(h, w_ref[:, 1024:1280]).astype(BF16)
    ga_ref[...] = _silu(_dot(h, w_ref[:, 1280:1792]))
    hy_ref[...] = _dot(h, w_ref[:, 1792:3328])
    gb_ref[...] = _silu(_dot(h, w_ref[:, 3328:3840]))


def _odd_in_kernel(x_ref, g_ref, w_ref, gq_ref, gkv_ref, wq_ref, wk_ref, wv_ref, cs_ref, sn_ref,
                   q_ref, kt_ref, v_ref, gc_ref, *, scale):
    h = _rms_rows(x_ref[...], g_ref[...]).astype(BF16)
    cs, sn = cs_ref[...], sn_ref[...]
    cq = _rms_rows(_dot(h, w_ref[:, 0:384]), gq_ref[...]).astype(BF16)
    ckv = _rms_rows(_dot(h, w_ref[:, 384:640]), gkv_ref[...]).astype(BF16)
    kr = _rope(_dot(h, w_ref[:, 640:768]), cs, sn)
    gc_ref[...] = _silu(_dot(h, w_ref[:, 768:1792]))
    q = _dot(cq, wq_ref[...])
    kn = _dot(ckv, wk_ref[...])
    for b in range(C_HEADS):
        sl = slice(b * LANES, (b + 1) * LANES)
        q_ref[:, sl] = (_rope(q[:, sl], cs, sn) * scale).astype(BF16)
        kt_ref[sl, :] = (kn[:, sl] + kr).T.astype(BF16)
    v_ref[...] = _dot(ckv, wv_ref[...]).astype(BF16)


def _attn_kernel(qa_ref, qb_ref, kta_ref, ktb_ref, v_ref, g_ref, o_ref, *, tk):
    tq = qa_ref.shape[0]
    nk = v_ref.shape[0] // tk

    def one_head(q_ref, kt_ref):
        q = q_ref[...]

        def body(i, carry):
            m, l, acc = carry
            off = pl.multiple_of(i * tk, tk)
            s = _dot(q, kt_ref[:, pl.ds(off, tk)])
            m_new = jnp.maximum(m, jnp.max(s, axis=-1, keepdims=True))
            alpha = jnp.exp(m - m_new)
            p = jnp.exp(s - m_new)
            l = alpha * l + jnp.sum(p, axis=-1, keepdims=True)
            acc = alpha * acc + _dot(p.astype(BF16), v_ref[pl.ds(off, tk), :])
            return m_new, l, acc

        init = (jnp.full((tq, 1), -jnp.inf, F32), jnp.zeros((tq, 1), F32), jnp.zeros((tq, LANES), F32))
        _, l, acc = lax.fori_loop(0, nk, body, init)
        return acc / l

    oa = one_head(qa_ref, kta_ref)
    ob = one_head(qb_ref, ktb_ref)
    lane = lax.broadcasted_iota(jnp.int32, (tq, LANES), 1)
    o_ref[...] = (jnp.where(lane < 64, oa, ob) * g_ref[...]).astype(o_ref.dtype)


def _hyena_pre_kernel(p_ref, prev_ref, next_ref, w_ref, b_ref, gb_ref, z_ref, x0g_ref, *, tiles_per_seq):
    i = pl.program_id(0)
    p = p_ref[...]
    tm = p.shape[0]
    row = lax.broadcasted_iota(jnp.int32, p.shape, 0)
    first = (i % tiles_per_seq) == 0
    last = (i % tiles_per_seq) == tiles_per_seq - 1
    halo_prev = jnp.where(first, 0.0, prev_ref[7:8, :])
    halo_next = jnp.where(last, 0.0, next_ref[0:1, :])
    p_prev = jnp.where(row == 0, halo_prev, pltpu.roll(p, 1, 0))
    p_next = jnp.where(row == tm - 1, halo_next, pltpu.roll(p, tm - 1, 0))
    u = p_prev * w_ref[0:1, :] + p * w_ref[1:2, :] + p_next * w_ref[2:3, :] + b_ref[...]
    c = HY_C
    z_ref[...] = (u[:, 2 * c:3 * c] * u[:, c:2 * c]).astype(BF16)
    x0g_ref[...] = u[:, 0:c] * gb_ref[...]


def _filter_kernel(feat_ref, w1h_ref, w1l_ref, b1_ref, f1_ref, w2h_ref, w2l_ref, b2_ref, f2_ref,
                   w3h_ref, w3l_ref, rate_ref, k_ref, l1_ref):
    feat = feat_ref[...]
    h = jnp.sin(f1_ref[...] * (_dot3(feat, w1h_ref[...], w1l_ref[...]) + b1_ref[...]))
    h = jnp.sin(f2_ref[...] * (_dot3(h, w2h_ref[...], w2l_ref[...]) + b2_ref[...]))
    k = _dot3(h, w3h_ref[...], w3l_ref[...])
    t = feat[:, 0:1]
    keep = feat[:, HY_EMB_DIM:HY_EMB_DIM + 1]
    k = k * jnp.exp(-t * rate_ref[...]) * keep
    k_ref[...] = k.astype(BF16)

    @pl.when(pl.program_id(0) == 0)
    def _():
        l1_ref[...] = jnp.zeros_like(l1_ref)

    l1_ref[...] += jnp.sum(jnp.abs(k), axis=0, keepdims=True)


def _left_dft_kernel(m_ref, x_ref, o_ref):
    o_ref[...] = _dot(m_ref[...].astype(BF16), x_ref[...]).astype(o_ref.dtype)


def _left_dft_gate_kernel(m_ref, x_ref, g_ref, o_ref):
    o_ref[...] = (_dot(m_ref[...].astype(BF16), x_ref[...]) * g_ref[...]).astype(o_ref.dtype)


def _spectral_kernel(ad_ref, af_ref, twr_ref, twi_ref, fb_ref, ib_ref, l1_ref, d_ref, o_ref):
    twr, twi = twr_ref[0], twi_ref[0]
    fb = fb_ref[...].astype(BF16)

    def stage2(a_ref):
        ar, ai = a_ref[0, 0].astype(F32), a_ref[1, 0].astype(F32)
        x = jnp.concatenate([ar * twr - ai * twi, ar * twi + ai * twr], axis=0).astype(BF16)
        xh = _dot(fb, x)
        return xh[:DFT_N2], xh[DFT_N2:]

    xr, xi = stage2(ad_ref)
    kr, ki = stage2(af_ref)
    inv_l1 = 1.0 / l1_ref[...]
    kr = kr * inv_l1 + d_ref[...]
    ki = ki * inv_l1
    y = jnp.concatenate([xr * kr - xi * ki, xr * ki + xi * kr], axis=0).astype(BF16)
    bt = _dot(ib_ref[...].astype(BF16), y)
    br, bi = bt[:DFT_N2], bt[DFT_N2:]
    o_ref[0, 0] = (br * twr + bi * twi).astype(o_ref.dtype)
    o_ref[1, 0] = (bi * twr - br * twi).astype(o_ref.dtype)


def _even_out_kernel(x_ref, ya_ref, yb_ref, w_ref, o_ref):
    o_ref[...] = x_ref[...] + _dot(ya_ref[...], w_ref[0:A_WIDTH, :]) + _dot(yb_ref[...], w_ref[A_WIDTH:, :])


def _odd_out_kernel(x_ref, y_ref, w_ref, g_ref, o_ref):
    o_ref[...] = _rms_rows(x_ref[...] + _dot(y_ref[...], w_ref[...]), g_ref[...])


def _full(shape):
    return pl.BlockSpec(shape, lambda *_: (0,) * len(shape))


def _rows(tm, width):
    return pl.BlockSpec((tm, width), lambda i: (i, 0))


def _attention(q, kt, v, gate, *, B, L, n_pairs, qa_col, qb_col, kta_row, ktb_row, v_col, tq, tk):
    T = B * L
    nq = L // tq
    qspec = lambda col: pl.BlockSpec((tq, LANES), lambda b, j, i: (b * nq + i, col(j)))
    ktspec = lambda row: pl.BlockSpec((LANES, L), lambda b, j, i: (row(j), b))
    return pl.pallas_call(
        functools.partial(_attn_kernel, tk=tk),
        grid=(B, n_pairs, nq),
        in_specs=[qspec(qa_col), qspec(qb_col), ktspec(kta_row), ktspec(ktb_row),
                  pl.BlockSpec((L, LANES), lambda b, j, i: (b, v_col(j))),
                  qspec(lambda j: j)],
        out_specs=qspec(lambda j: j),
        out_shape=jax.ShapeDtypeStruct((T, n_pairs * LANES), BF16),
        compiler_params=_cparams(("parallel", "parallel", "arbitrary")),
        name="attention",
    )(q, q, kt, kt, v, gate)


def _left_dft(mat, x, gate=None, *, out_dtype, tw, name):
    m, k = mat.shape
    width = x.shape[1]
    xspec = pl.BlockSpec((k, tw), lambda i: (0, i))
    ospec = pl.BlockSpec((m, tw), lambda i: (0, i))
    args, specs, body = [mat, x], [_full((m, k)), xspec], _left_dft_kernel
    if gate is not None:
        args.append(gate); specs.append(ospec); body = _left_dft_gate_kernel
    return pl.pallas_call(
        body, grid=(width // tw,), in_specs=specs, out_specs=ospec,
        out_shape=jax.ShapeDtypeStruct((m, width), out_dtype),
        compiler_params=_cparams(("parallel",)), name=name,
    )(*args)


def kernel(x, e_norm, e_w_in, e_q_norm, e_k_norm, e_conv_w, e_conv_b, e_filt_w1, e_filt_b1, e_filt_f1,
           e_filt_w2, e_filt_b2, e_filt_f2, e_filt_w3, e_hy_d, e_w_out,
           o_norm, o_w_in, o_q_a_norm, o_w_qb, o_kv_a_norm, o_w_kvb, o_w_out, final_norm):
    B, L, D = x.shape
    assert D == D_MODEL and B == 2 and L % 1024 == 0 and L % GRID_W == 0
    T = B * L
    tm = 512
    tq = 256
    tk = 512
    nt = T // tm
    tiles_per_seq = L // tm
    xt = x.reshape(T, D)
    row2 = lambda a: a.reshape(1, -1).astype(F32)

    idx, keep = _even_columns()
    w_e = _gather_cols(e_w_in[0], idx, keep).astype(BF16)
    ne = w_e.shape[1]
    cs_a, sn_a = _rope_tables_a(L)
    gidx = _head_gain_index()
    table = pl.BlockSpec((tm, LANES), lambda i: (i % tiles_per_seq, 0))
    q_a, kt_a, v_a, ga, hy, gbs = pl.pallas_call(
        functools.partial(_even_in_kernel, scale=A_HEAD_DIM ** -0.5),
        grid=(nt,),
        in_specs=[_rows(tm, D), _full((1, D)), _full((D, ne)), _full((LANES, LANES)),
                  _full((1, LANES)), _full((1, LANES)), table, table],
        out_specs=[_rows(tm, 512), pl.BlockSpec((512, tm), lambda i: (0, i)), _rows(tm, 256),
                   _rows(tm, 512), _rows(tm, 3 * HY_C), _rows(tm, HY_C)],
        out_shape=[jax.ShapeDtypeStruct((T, 512), BF16), jax.ShapeDtypeStruct((512, T), BF16),
                   jax.ShapeDtypeStruct((T, 256), BF16), jax.ShapeDtypeStruct((T, 512), F32),
                   jax.ShapeDtypeStruct((T, 3 * HY_C), F32), jax.ShapeDtypeStruct((T, HY_C), F32)],
        compiler_params=_cparams(("parallel",)), name="even_in",
    )(xt, row2(e_norm[0]), w_e, jnp.asarray(_segment_mean_matrix(), BF16),
      row2(e_q_norm[0][gidx]), row2(e_k_norm[0][gidx]), jnp.asarray(cs_a), jnp.asarray(sn_a))

    ya = _attention(q_a, kt_a, v_a, ga, B=B, L=L, n_pairs=A_HEADS // 2,
                    qa_col=lambda j: j, qb_col=lambda j: j,
                    kta_row=lambda j: 2 * (j // 2), ktb_row=lambda j: 2 * (j // 2) + 1,
                    v_col=lambda j: j // 2, tq=tq, tk=tk)

    c = HY_C
    sub = tm // 8
    z, x0g = pl.pallas_call(
        functools.partial(_hyena_pre_kernel, tiles_per_seq=tiles_per_seq),
        grid=(nt,),
        in_specs=[_rows(tm, 3 * c),
                  pl.BlockSpec((8, 3 * c), lambda i: (jnp.maximum(i * sub - 1, 0), 0)),
                  pl.BlockSpec((8, 3 * c), lambda i: (jnp.minimum((i + 1) * sub, T // 8 - 1), 0)),
                  _full((3, 3 * c)), _full((1, 3 * c)), _rows(tm, c)],
        out_specs=[_rows(tm, c), _rows(tm, c)],
        out_shape=[jax.ShapeDtypeStruct((T, c), BF16), jax.ShapeDtypeStruct((T, c), F32)],
        compiler_params=_cparams(("parallel",)), name="hyena_pre",
    )(hy, hy, hy, e_conv_w[0].astype(F32), row2(e_conv_b[0]), gbs)

    dft = _dft_constants(L)
    n1 = dft["n1"]
    bf = lambda a: jnp.asarray(a, F32)
    w1 = jnp.zeros((FEAT_W, HY_HIDDEN), F32).at[:HY_EMB_DIM].set(e_filt_w1[0])
    tf = 1024
    kfilt, l1 = pl.pallas_call(
        _filter_kernel,
        grid=(2 * L // tf,),
        in_specs=[_rows(tf, FEAT_W)] + [_full((FEAT_W, HY_HIDDEN))] * 2 + [_full((1, HY_HIDDEN))] * 2
                 + [_full((HY_HIDDEN, HY_HIDDEN))] * 2 + [_full((1, HY_HIDDEN))] * 2
                 + [pl.BlockSpec((HY_HIDDEN, c), lambda i: (0, i // (L // tf)))] * 2 + [_full((1, c))],
        out_specs=[_rows(tf, c), _full((1, c))],
        out_shape=[jax.ShapeDtypeStruct((2 * L, c), BF16), jax.ShapeDtypeStruct((1, c), F32)],
        compiler_params=_cparams(("arbitrary",)), name="hyena_filter",
    )(jnp.asarray(_filter_features(L)), *_split_bf16(w1), row2(e_filt_b1[0]), row2(e_filt_f1[0]),
      *_split_bf16(e_filt_w2[0]), row2(e_filt_b2[0]), row2(e_filt_f2[0]),
      *_split_bf16(e_filt_w3[0]), jnp.asarray(_decay_rates()))

    width = DFT_N2 * c
    tw = 4096
    a_data = _left_dft(bf(dft["fd"]), z.reshape(n1, width), out_dtype=BF16, tw=tw, name="dft_data")
    a_filt = _left_dft(bf(dft["ff"]), kfilt.reshape(n1, width), out_dtype=BF16, tw=tw, name="dft_filter")
    slab = pl.BlockSpec((2, 1, DFT_N2, c), lambda i: (0, i, 0, 0))
    twspec = pl.BlockSpec((1, DFT_N2, 1), lambda i: (i, 0, 0))
    bspec = pl.pallas_call(
        _spectral_kernel,
        grid=(n1,),
        in_specs=[slab, slab, twspec, twspec, _full((2 * DFT_N2, 2 * DFT_N2)), _full((2 * DFT_N2, 2 * DFT_N2)),
                  _full((1, c)), _full((1, c))],
        out_specs=slab,
        out_shape=jax.ShapeDtypeStruct((2, n1, DFT_N2, c), BF16),
        compiler_params=_cparams(("parallel",)), name="hyena_spectral",
    )(a_data.reshape(2, n1, DFT_N2, c), a_filt.reshape(2, n1, DFT_N2, c),
      jnp.asarray(dft["twr"]), jnp.asarray(dft["twi"]), bf(dft["fb"]), bf(dft["ib"]), l1, row2(e_hy_d[0]))
    yb = _left_dft(bf(dft["if2"]), bspec.reshape(2 * n1, width), x0g.reshape(n1, width),
                   out_dtype=BF16, tw=tw, name="dft_inverse").reshape(T, c)

    x1 = pl.pallas_call(
        _even_out_kernel, grid=(nt,),
        in_specs=[_rows(tm, D), _rows(tm, A_WIDTH), _rows(tm, c), _full((A_WIDTH + c, D))],
        out_specs=_rows(tm, D), out_shape=jax.ShapeDtypeStruct((T, D), F32),
        compiler_params=_cparams(("parallel",)), name="even_out",
    )(xt, ya, yb, e_w_out[0].astype(BF16))

    in_idx, in_keep, q_idx, q_keep, k_idx, k_keep, v_idx = _odd_columns()
    w_o = _gather_cols(o_w_in[0], in_idx, in_keep).astype(BF16)
    w_q = _gather_cols(o_w_qb[0], q_idx, q_keep).astype(BF16)
    w_k = _gather_cols(o_w_kvb[0], k_idx, k_keep).astype(BF16)
    w_v = _gather_cols(o_w_kvb[0], v_idx, np.ones(len(v_idx))).astype(BF16)
    cs_c, sn_c = _rope_tables_c(L)
    hq = C_HEADS * LANES
    q_c, kt_c, v_c, gc = pl.pallas_call(
        functools.partial(_odd_in_kernel, scale=(C_NOPE + C_ROPE) ** -0.5),
        grid=(nt,),
        in_specs=[_rows(tm, D), _full((1, D)), _full((D, w_o.shape[1])), _full((1, C_Q_RANK)),
                  _full((1, C_KV_RANK)), _full((C_Q_RANK, hq)), _full((C_KV_RANK, hq)),
                  _full((C_KV_RANK, C_WIDTH)), table, table],
        out_specs=[_rows(tm, hq), pl.BlockSpec((hq, tm), lambda i: (0, i)), _rows(tm, C_WIDTH),
                   _rows(tm, C_WIDTH)],
        out_shape=[jax.ShapeDtypeStruct((T, hq), BF16), jax.ShapeDtypeStruct((hq, T), BF16),
                   jax.ShapeDtypeStruct((T, C_WIDTH), BF16), jax.ShapeDtypeStruct((T, C_WIDTH), F32)],
        compiler_params=_cparams(("parallel",)), name="odd_in",
    )(x1, row2(o_norm[0]), w_o, row2(o_q_a_norm[0]), row2(o_kv_a_norm[0]), w_q, w_k, w_v,
      jnp.asarray(cs_c), jnp.asarray(sn_c))

    yc = _attention(q_c, kt_c, v_c, gc, B=B, L=L, n_pairs=C_HEADS // 2,
                    qa_col=lambda j: 2 * j, qb_col=lambda j: 2 * j + 1,
                    kta_row=lambda j: 2 * j, ktb_row=lambda j: 2 * j + 1,
                    v_col=lambda j: j, tq=tq, tk=tk)

    out = pl.pallas_call(
        _odd_out_kernel, grid=(nt,),
        in_specs=[_rows(tm, D), _rows(tm, C_WIDTH), _full((C_WIDTH, D)), _full((1, D))],
        out_specs=_rows(tm, D), out_shape=jax.ShapeDtypeStruct((T, D), F32),
        compiler_params=_cparams(("parallel",)), name="odd_out",
    )(x1, yc, o_w_out[0].astype(BF16), row2(final_norm))
    return out.reshape(B, L, D)
```

```python
import functools
import math

import numpy as np
import jax
import jax.numpy as jnp
from jax import lax
from jax.experimental import pallas as pl
from jax.experimental.pallas import tpu as pltpu

F32 = jnp.float32
BF16 = jnp.bfloat16

D_MODEL = 1024
GRID_W = 64
EPS = 1e-6
ROPE_THETA = 10000.0

A_HEADS, A_KV_HEADS, A_HEAD_DIM = 8, 2, 64
A_WIDTH = A_HEADS * A_HEAD_DIM
HY_C = D_MODEL // 2
HY_EMB_BANDS = 16
HY_EMB_DIM = 1 + 2 * HY_EMB_BANDS
HY_HIDDEN = 64
HY_DECAY_TARGET, HY_FAST_DECAY, HY_SLOW_DECAY = 1e-2, 0.3, 1.5
EVEN_IN = 3328

C_HEADS, C_NOPE, C_ROPE, C_V = 16, 64, 32, 64
C_Q_RANK, C_KV_RANK = 384, 256
C_WIDTH = C_HEADS * C_V

LANES = 128
DFT_N2 = 128
FEAT_W = 64
VMEM_LIMIT = 56 * 1024 * 1024
LOG2E = math.log2(math.e)
ATTN_ROW_CHUNK = 16
ATTN_FLAGS = None


def _cparams(sem, flags=None):
    return pltpu.CompilerParams(dimension_semantics=sem, vmem_limit_bytes=VMEM_LIMIT, flags=flags)


def _silu(x):
    return x * (1.0 / (1.0 + jnp.exp(-x)))


def _dot(a, b):
    return jnp.dot(a, b, preferred_element_type=F32)


def _split_bf16(a):
    hi = a.astype(BF16)
    return hi, (a - hi.astype(F32)).astype(BF16)


def _dot3(a, w_hi, w_lo):
    a_hi, a_lo = _split_bf16(a)
    return _dot(a_hi, w_hi) + (_dot(a_hi, w_lo) + _dot(a_lo, w_hi))


def _axial_angles(L, dim):
    rows = L // GRID_W
    row = np.repeat(np.arange(rows), GRID_W).astype(np.float64)
    col = np.tile(np.arange(GRID_W), rows).astype(np.float64)
    n = dim // 4
    inv = ROPE_THETA ** (-np.arange(n, dtype=np.float64) / n)
    return np.concatenate([row[:, None] * inv, col[:, None] * inv], axis=-1)


def _rope_tables_a(L):
    ang = _axial_angles(L, A_HEAD_DIM)
    lane = np.arange(LANES)
    j = lane % 32
    sign = np.where(lane // 64 == 0, -1.0, 1.0)
    return (np.cos(ang)[:, j].astype(np.float32), (np.sin(ang)[:, j] * sign).astype(np.float32))


def _rope_tables_c(L):
    ang = _axial_angles(L, C_ROPE)
    cs = np.zeros((L, LANES)); sn = np.zeros((L, LANES))
    cs[:, 0:32] = 1.0; cs[:, 64:96] = 1.0
    cs[:, 32:48] = np.cos(ang); cs[:, 96:112] = np.cos(ang)
    sn[:, 32:48] = -np.sin(ang); sn[:, 96:112] = np.sin(ang)
    return cs.astype(np.float32), sn.astype(np.float32)


def _even_columns():
    q0, k0, v0, rest0 = 0, 512, 640, 768
    idx, keep = [], []
    lane = np.arange(LANES)
    part, hs, j = lane // 64, (lane % 64) // 32, lane % 32
    for pair in range(A_HEADS // 2):
        idx.append(q0 + (2 * pair + hs) * 64 + 32 * part + j); keep.append(np.ones(LANES))
    for g in range(A_KV_HEADS):
        for var in range(2):
            idx.append(k0 + g * 64 + 32 * part + j); keep.append((hs == var).astype(np.float64))
    for g in range(A_KV_HEADS):
        idx.append(v0 + g * 64 + lane % 64); keep.append((lane < 64).astype(np.float64))
    idx.append(np.arange(rest0, EVEN_IN)); keep.append(np.ones(EVEN_IN - rest0))
    return np.concatenate(idx).astype(np.int32), np.concatenate(keep).astype(np.float32)


def _head_gain_index():
    lane = np.arange(LANES)
    return (32 * (lane // 64) + lane % 32).astype(np.int32)


def _segment_mean_matrix():
    lane = np.arange(LANES)
    cls = (lane % 64) // 32
    return ((cls[:, None] == cls[None, :]) / float(A_HEAD_DIM)).astype(np.float32)


def _mla_head_lane_sources():
    src = -np.ones(LANES, np.int64)
    src[0:32] = np.arange(32)
    src[32:48] = 64 + np.arange(16)
    src[64:96] = 32 + np.arange(32)
    src[96:112] = 80 + np.arange(16)
    return src


def _odd_columns():
    src = _mla_head_lane_sources()
    kr = np.where(src >= 64, 640 + (src - 64), 0)
    kr_keep = (src >= 64).astype(np.float32)
    in_idx = np.concatenate([np.arange(0, 640), kr, np.arange(672, 1696)]).astype(np.int32)
    in_keep = np.concatenate([np.ones(640), kr_keep, np.ones(1024)]).astype(np.float32)
    q_idx, q_keep, k_idx, k_keep, v_idx, v_keep = [], [], [], [], [], []
    lane = np.arange(LANES)
    for h in range(C_HEADS):
        q_idx.append(np.where(src >= 0, h * (C_NOPE + C_ROPE) + src, 0)); q_keep.append(src >= 0)
        nope = (src >= 0) & (src < 64)
        k_idx.append(np.where(nope, h * (C_NOPE + C_V) + src, 0)); k_keep.append(nope)
        v_idx.append(h * (C_NOPE + C_V) + C_NOPE + lane % C_V); v_keep.append(lane < C_V)
    cat = lambda xs, dt: np.concatenate(xs).astype(dt)
    return (in_idx, in_keep, cat(q_idx, np.int32), cat(q_keep, np.float32),
            cat(k_idx, np.int32), cat(k_keep, np.float32), cat(v_idx, np.int32), cat(v_keep, np.float32))


def _gather_cols(w, idx, keep):
    pieces, start, n = [], 0, len(idx)
    while start < n:
        end = start + 1
        while end < n and keep[end] == keep[start] and (keep[start] == 0 or idx[end] == idx[end - 1] + 1):
            end += 1
        if keep[start]:
            pieces.append(w[:, int(idx[start]):int(idx[start]) + end - start])
        else:
            pieces.append(jnp.zeros((w.shape[0], end - start), w.dtype))
        start = end
    return jnp.concatenate(pieces, axis=1)


def _dft_constants(L):
    n1 = 2 * L // DFT_N2
    n = n1 * DFT_N2
    h = n1 // 2
    a1 = -2.0 * np.pi * np.outer(np.arange(n1), np.arange(n1)) / n1
    f1r, f1i = np.cos(a1), np.sin(a1)
    a2 = -2.0 * np.pi * np.outer(np.arange(DFT_N2), np.arange(DFT_N2)) / DFT_N2
    f2r, f2i = np.cos(a2), np.sin(a2)
    at = -2.0 * np.pi * np.outer(np.arange(n1), np.arange(DFT_N2)) / n
    fd = np.block([[f1r[:, :h], -f1i[:, :h]], [f1i[:, :h], f1r[:, :h]]])
    ff = np.concatenate([f1r, f1i], axis=0)
    fb = np.block([[f2r, -f2i], [f2i, f2r]])
    ib = np.block([[f2r, f2i], [-f2i, f2r]])
    if2 = np.block([[f1r[:h], f1i[:h]], [-f1i[:h], f1r[:h]]]) / n
    return dict(n1=n1, fd=fd, ff=ff, fb=fb, ib=ib, if2=if2,
                twr=np.cos(at)[:, :, None].astype(np.float32),
                twi=np.sin(at)[:, :, None].astype(np.float32))


def _filter_features(L):
    r = np.arange(2 * L)
    lag = np.where(r < L, r, 2 * L - r)
    lag = np.where(r == L, 0, lag)
    t = lag / float(L - 1)
    w = 2.0 * np.pi * lag / float(L)
    bands = np.linspace(1e-4, HY_EMB_BANDS - 1, HY_EMB_BANDS)
    feat = np.zeros((2 * L, FEAT_W))
    feat[:, 0] = t
    feat[:, 1:1 + HY_EMB_BANDS] = np.cos(bands[None] * w[:, None])
    feat[:, 1 + HY_EMB_BANDS:HY_EMB_DIM] = -np.sin(bands[None] * w[:, None])
    feat[:, HY_EMB_DIM] = (r != L)
    return feat.astype(np.float32)


def _decay_rates():
    lo = math.log(HY_DECAY_TARGET) / HY_SLOW_DECAY
    hi = math.log(HY_DECAY_TARGET) / HY_FAST_DECAY
    return np.abs(np.linspace(lo, hi, HY_C)).astype(np.float32)[None]


def _rms_rows(x, gain):
    return x * lax.rsqrt(jnp.mean(x * x, axis=-1, keepdims=True) + EPS) * gain


def _rope(y, cs, sn):
    return y * cs + pltpu.roll(y, 64, 1) * sn


def _with_ones_half(v):
    lane = lax.broadcasted_iota(jnp.int32, v.shape, 1)
    return jnp.where(lane % LANES < 64, v, 1.0)


def _even_in_kernel(x_ref, g_ref, w_ref, m_ref, gq_ref, gk_ref, cs_ref, sn_ref,
                    q_ref, kt_ref, v_ref, ga_ref, hy_ref, gb_ref, *, scale):
    h = _rms_rows(x_ref[...], g_ref[...]).astype(BF16)
    cs, sn, m = cs_ref[...], sn_ref[...], m_ref[...]

    def norm_rope(p, gain):
        ms = _dot((p * p).astype(BF16), m)
        return _rope(p * lax.rsqrt(ms + EPS) * gain, cs, sn)

    pq = _dot(h, w_ref[:, 0:512])
    pk = _dot(h, w_ref[:, 512:1024])
    for b in range(4):
        sl = slice(b * LANES, (b + 1) * LANES)
        q_ref[:, sl] = (norm_rope(pq[:, sl], gq_ref[...]) * scale).astype(BF16)
        kt_ref[sl, :] = norm_rope(pk[:, sl], gk_ref[...]).T.astype(BF16)
    v_ref[...] = _with_ones_half(_dot(h, w_ref[:, 1024:1280])).astype(BF16)
    ga_ref[...] = _silu(_dot(h, w_ref[:, 1280:1792]))
    hy_ref[...] = _dot(h, w_ref[:, 1792:3328])
    gb_ref[...] = _silu(_dot(h, w_ref[:, 3328:3840]))


def _odd_in_kernel(x_ref, g_ref, w_ref, gq_ref, gkv_ref, wq_ref, wk_ref, wv_ref, cs_ref, sn_ref,
                   q_ref, kt_ref, v_ref, gc_ref, *, scale):
    h = _rms_rows(x_ref[...], g_ref[...]).astype(BF16)
    cs, sn = cs_ref[...], sn_ref[...]
    cq = _rms_rows(_dot(h, w_ref[:, 0:384]), gq_ref[...]).astype(BF16)
    ckv = _rms_rows(_dot(h, w_ref[:, 384:640]), gkv_ref[...]).astype(BF16)
    kr = _rope(_dot(h, w_ref[:, 640:768]), cs, sn)
    gc_ref[...] = _silu(_dot(h, w_ref[:, 768:1792]))
    q = _dot(cq, wq_ref[...])
    kn = _dot(ckv, wk_ref[...])
    for b in range(C_HEADS):
        sl = slice(b * LANES, (b + 1) * LANES)
        q_ref[:, sl] = (_rope(q[:, sl], cs, sn) * scale).astype(BF16)
        kt_ref[sl, :] = (kn[:, sl] + kr).T.astype(BF16)
    v_ref[...] = _with_ones_half(_dot(ckv, wv_ref[...])).astype(BF16)


def _attn_kernel(qa_ref, qb_ref, kta_ref, ktb_ref, va_ref, vb_ref, g_ref, o_ref, *scratch, tk):
    tq = qa_ref.shape[0]
    nk = va_ref.shape[0] // tk
    nb = tk // LANES
    assert nk % 2 == 0
    grid22 = lambda refs: ((refs[0], refs[1]), (refs[2], refs[3]))
    s_scr, p_scr, m_scr, a_scr = (grid22(scratch[4 * i:4 * i + 4]) for i in range(4))
    acc_scr = scratch[16:18]
    qs = (qa_ref[...], qb_ref[...])
    kts = (kta_ref, ktb_ref)
    vs = (va_ref, vb_ref)
    for h in range(2):
        m_scr[1][h][...] = jnp.full((tq, LANES), -jnp.inf, F32)
        acc_scr[h][...] = jnp.zeros((tq, LANES), F32)

    def score_tile(t, c):
        off = pl.multiple_of(t * tk, tk)
        for h in range(2):
            s = _dot(qs[h], kts[h][:, pl.ds(off, tk)])
            s_scr[c][h][...] = s
            tile_max = functools.reduce(jnp.maximum, [s[:, j * LANES:(j + 1) * LANES] for j in range(nb)])
            m_prev = m_scr[1 - c][h][...]
            m_new = jnp.maximum(m_prev, jnp.max(tile_max, axis=-1, keepdims=True))
            m_scr[c][h][...] = m_new
            a_scr[c][h][...] = jnp.exp2(m_prev - m_new)

    def consume(t, c):
        off = pl.multiple_of(t * tk, tk)
        for h in range(2):
            for r in range(0, tq, ATTN_ROW_CHUNK):
                rows = slice(r, r + ATTN_ROW_CHUNK)
                m_new = m_scr[c][h][rows]
                for j in range(nb):
                    cols = slice(j * LANES, (j + 1) * LANES)
                    p_scr[c][h][rows, cols] = jnp.exp2(s_scr[c][h][rows, cols] - m_new).astype(BF16)
            pv = _dot(p_scr[c][h][...], vs[h][pl.ds(off, tk), :])
            acc_scr[h][...] = a_scr[c][h][...] * acc_scr[h][...] + pv

    score_tile(0, 0)

    def body(j, carry):
        t = 2 * j
        score_tile(t + 1, 1)
        consume(t, 0)
        score_tile(t + 2, 0)
        consume(t + 1, 1)
        return carry

    lax.fori_loop(0, nk // 2 - 1, body, 0)
    score_tile(nk - 1, 1)
    consume(nk - 2, 0)
    consume(nk - 1, 1)
    out = [acc_scr[h][...] / pltpu.roll(acc_scr[h][...], 64, 1) for h in range(2)]
    lane = lax.broadcasted_iota(jnp.int32, (tq, LANES), 1)
    o_ref[...] = (jnp.where(lane < 64, out[0], pltpu.roll(out[1], 64, 1)) * g_ref[...]).astype(o_ref.dtype)


def _hyena_pre_kernel(p_ref, prev_ref, next_ref, w_ref, b_ref, gb_ref, z_ref, x0g_ref, *, tiles_per_seq):
    i = pl.program_id(0)
    p = p_ref[...]
    tm = p.shape[0]
    row = lax.broadcasted_iota(jnp.int32, p.shape, 0)
    first = (i % tiles_per_seq) == 0
    last = (i % tiles_per_seq) == tiles_per_seq - 1
    halo_prev = jnp.where(first, 0.0, prev_ref[7:8, :])
    halo_next = jnp.where(last, 0.0, next_ref[0:1, :])
    p_prev = jnp.where(row == 0, halo_prev, pltpu.roll(p, 1, 0))
    p_next = jnp.where(row == tm - 1, halo_next, pltpu.roll(p, tm - 1, 0))
    u = p_prev * w_ref[0:1, :] + p * w_ref[1:2, :] + p_next * w_ref[2:3, :] + b_ref[...]
    c = HY_C
    z_ref[...] = (u[:, 2 * c:3 * c] * u[:, c:2 * c]).astype(BF16)
    x0g_ref[...] = u[:, 0:c] * gb_ref[...]


def _filter_kernel(feat_ref, w1h_ref, w1l_ref, b1_ref, f1_ref, w2h_ref, w2l_ref, b2_ref, f2_ref,
                   w3h_ref, w3l_ref, rate_ref, k_ref, l1_ref):
    feat = feat_ref[...]
    h = jnp.sin(f1_ref[...] * (_dot3(feat, w1h_ref[...], w1l_ref[...]) + b1_ref[...]))
    h = jnp.sin(f2_ref[...] * (_dot3(h, w2h_ref[...], w2l_ref[...]) + b2_ref[...]))
    k = _dot3(h, w3h_ref[...], w3l_ref[...])
    t = feat[:, 0:1]
    keep = feat[:, HY_EMB_DIM:HY_EMB_DIM + 1]
    k = k * jnp.exp(-t * rate_ref[...]) * keep
    k_ref[...] = k.astype(BF16)

    @pl.when(pl.program_id(0) == 0)
    def _():
        l1_ref[...] = jnp.zeros_like(l1_ref)

    l1_ref[...] += jnp.sum(jnp.abs(k), axis=0, keepdims=True)


def _left_dft_kernel(m_ref, x_ref, o_ref):
    o_ref[...] = _dot(m_ref[...].astype(BF16), x_ref[...]).astype(o_ref.dtype)


def _left_dft_gate_kernel(m_ref, x_ref, g_ref, o_ref):
    o_ref[...] = (_dot(m_ref[...].astype(BF16), x_ref[...]) * g_ref[...]).astype(o_ref.dtype)


def _spectral_kernel(ad_ref, af_ref, twr_ref, twi_ref, fb_ref, ib_ref, l1_ref, d_ref, o_ref):
    twr, twi = twr_ref[0], twi_ref[0]
    fb = fb_ref[...].astype(BF16)

    def stage2(a_ref):
        ar, ai = a_ref[0, 0].astype(F32), a_ref[1, 0].astype(F32)
        x = jnp.concatenate([ar * twr - ai * twi, ar * twi + ai * twr], axis=0).astype(BF16)
        xh = _dot(fb, x)
        return xh[:DFT_N2], xh[DFT_N2:]

    xr, xi = stage2(ad_ref)
    kr, ki = stage2(af_ref)
    inv_l1 = 1.0 / l1_ref[...]
    kr = kr * inv_l1 + d_ref[...]
    ki = ki * inv_l1
    y = jnp.concatenate([xr * kr - xi * ki, xr * ki + xi * kr], axis=0).astype(BF16)
    bt = _dot(ib_ref[...].astype(BF16), y)
    br, bi = bt[:DFT_N2], bt[DFT_N2:]
    o_ref[0, 0] = (br * twr + bi * twi).astype(o_ref.dtype)
    o_ref[1, 0] = (bi * twr - br * twi).astype(o_ref.dtype)


def _even_out_kernel(x_ref, ya_ref, yb_ref, w_ref, o_ref):
    o_ref[...] = x_ref[...] + _dot(ya_ref[...], w_ref[0:A_WIDTH, :]) + _dot(yb_ref[...], w_ref[A_WIDTH:, :])


def _odd_out_kernel(x_ref, y_ref, w_ref, g_ref, o_ref):
    o_ref[...] = _rms_rows(x_ref[...] + _dot(y_ref[...], w_ref[...]), g_ref[...])


def _full(shape):
    return pl.BlockSpec(shape, lambda *_: (0,) * len(shape))


def _rows(tm, width):
    return pl.BlockSpec((tm, width), lambda i: (i, 0))


def _attention(q, kt, v, gate, *, B, L, n_pairs, qa_col, qb_col, kta_row, ktb_row, va_col, vb_col, tq, tk):
    T = B * L
    nq = L // tq
    qspec = lambda col: pl.BlockSpec((tq, LANES), lambda b, j, i: (b * nq + i, col(j)))
    ktspec = lambda row: pl.BlockSpec((LANES, L), lambda b, j, i: (row(j), b))
    vspec = lambda col: pl.BlockSpec((L, LANES), lambda b, j, i: (b, col(j)))
    return pl.pallas_call(
        functools.partial(_attn_kernel, tk=tk),
        grid=(B, n_pairs, nq),
        in_specs=[qspec(qa_col), qspec(qb_col), ktspec(kta_row), ktspec(ktb_row),
                  vspec(va_col), vspec(vb_col), qspec(lambda j: j)],
        out_specs=qspec(lambda j: j),
        out_shape=jax.ShapeDtypeStruct((T, n_pairs * LANES), BF16),
        scratch_shapes=([pltpu.VMEM((tq, tk), F32)] * 4 + [pltpu.VMEM((tq, tk), BF16)] * 4
                        + [pltpu.VMEM((tq, LANES), F32)] * 10),
        compiler_params=_cparams(("parallel", "parallel", "arbitrary"), ATTN_FLAGS),
        name="attention",
    )(q, q, kt, kt, v, v, gate)


def _left_dft(mat, x, gate=None, *, out_dtype, tw, name):
    m, k = mat.shape
    width = x.shape[1]
    xspec = pl.BlockSpec((k, tw), lambda i: (0, i))
    ospec = pl.BlockSpec((m, tw), lambda i: (0, i))
    args, specs, body = [mat, x], [_full((m, k)), xspec], _left_dft_kernel
    if gate is not None:
        args.append(gate); specs.append(ospec); body = _left_dft_gate_kernel
    return pl.pallas_call(
        body, grid=(width // tw,), in_specs=specs, out_specs=ospec,
        out_shape=jax.ShapeDtypeStruct((m, width), out_dtype),
        compiler_params=_cparams(("parallel",)), name=name,
    )(*args)


def kernel(x, e_norm, e_w_in, e_q_norm, e_k_norm, e_conv_w, e_conv_b, e_filt_w1, e_filt_b1, e_filt_f1,
           e_filt_w2, e_filt_b2, e_filt_f2, e_filt_w3, e_hy_d, e_w_out,
           o_norm, o_w_in, o_q_a_norm, o_w_qb, o_kv_a_norm, o_w_kvb, o_w_out, final_norm):
    B, L, D = x.shape
    assert D == D_MODEL and B == 2 and L % 1024 == 0 and L % GRID_W == 0
    T = B * L
    tm = 512
    tq = 1024
    tk = 1024
    nt = T // tm
    tiles_per_seq = L // tm
    xt = x.reshape(T, D)
    row2 = lambda a: a.reshape(1, -1).astype(F32)

    idx, keep = _even_columns()
    w_e = _gather_cols(e_w_in[0], idx, keep).astype(BF16)
    ne = w_e.shape[1]
    cs_a, sn_a = _rope_tables_a(L)
    gidx = _head_gain_index()
    table = pl.BlockSpec((tm, LANES), lambda i: (i % tiles_per_seq, 0))
    q_a, kt_a, v_a, ga, hy, gbs = pl.pallas_call(
        functools.partial(_even_in_kernel, scale=A_HEAD_DIM ** -0.5 * LOG2E),
        grid=(nt,),
        in_specs=[_rows(tm, D), _full((1, D)), _full((D, ne)), _full((LANES, LANES)),
                  _full((1, LANES)), _full((1, LANES)), table, table],
        out_specs=[_rows(tm, 512), pl.BlockSpec((512, tm), lambda i: (0, i)), _rows(tm, 256),
                   _rows(tm, 512), _rows(tm, 3 * HY_C), _rows(tm, HY_C)],
        out_shape=[jax.ShapeDtypeStruct((T, 512), BF16), jax.ShapeDtypeStruct((512, T), BF16),
                   jax.ShapeDtypeStruct((T, 256), BF16), jax.ShapeDtypeStruct((T, 512), F32),
                   jax.ShapeDtypeStruct((T, 3 * HY_C), F32), jax.ShapeDtypeStruct((T, HY_C), F32)],
        compiler_params=_cparams(("parallel",)), name="even_in",
    )(xt, row2(e_norm[0]), w_e, jnp.asarray(_segment_mean_matrix(), BF16),
      row2(e_q_norm[0][gidx]), row2(e_k_norm[0][gidx]), jnp.asarray(cs_a), jnp.asarray(sn_a))

    ya = _attention(q_a, kt_a, v_a, ga, B=B, L=L, n_pairs=A_HEADS // 2,
                    qa_col=lambda j: j, qb_col=lambda j: j,
                    kta_row=lambda j: 2 * (j // 2), ktb_row=lambda j: 2 * (j // 2) + 1,
                    va_col=lambda j: j // 2, vb_col=lambda j: j // 2, tq=tq, tk=tk)

    c = HY_C
    sub = tm // 8
    z, x0g = pl.pallas_call(
        functools.partial(_hyena_pre_kernel, tiles_per_seq=tiles_per_seq),
        grid=(nt,),
        in_specs=[_rows(tm, 3 * c),
                  pl.BlockSpec((8, 3 * c), lambda i: (jnp.maximum(i * sub - 1, 0), 0)),
                  pl.BlockSpec((8, 3 * c), lambda i: (jnp.minimum((i + 1) * sub, T // 8 - 1), 0)),
                  _full((3, 3 * c)), _full((1, 3 * c)), _rows(tm, c)],
        out_specs=[_rows(tm, c), _rows(tm, c)],
        out_shape=[jax.ShapeDtypeStruct((T, c), BF16), jax.ShapeDtypeStruct((T, c), F32)],
        compiler_params=_cparams(("parallel",)), name="hyena_pre",
    )(hy, hy, hy, e_conv_w[0].astype(F32), row2(e_conv_b[0]), gbs)

    dft = _dft_constants(L)
    n1 = dft["n1"]
    bf = lambda a: jnp.asarray(a, F32)
    w1 = jnp.zeros((FEAT_W, HY_HIDDEN), F32).at[:HY_EMB_DIM].set(e_filt_w1[0])
    tf = 1024
    kfilt, l1 = pl.pallas_call(
        _filter_kernel,
        grid=(2 * L // tf,),
        in_specs=[_rows(tf, FEAT_W)] + [_full((FEAT_W, HY_HIDDEN))] * 2 + [_full((1, HY_HIDDEN))] * 2
                 + [_full((HY_HIDDEN, HY_HIDDEN))] * 2 + [_full((1, HY_HIDDEN))] * 2
                 + [pl.BlockSpec((HY_HIDDEN, c), lambda i: (0, i // (L // tf)))] * 2 + [_full((1, c))],
        out_specs=[_rows(tf, c), _full((1, c))],
        out_shape=[jax.ShapeDtypeStruct((2 * L, c), BF16), jax.ShapeDtypeStruct((1, c), F32)],
        compiler_params=_cparams(("arbitrary",)), name="hyena_filter",
    )(jnp.asarray(_filter_features(L)), *_split_bf16(w1), row2(e_filt_b1[0]), row2(e_filt_f1[0]),
      *_split_bf16(e_filt_w2[0]), row2(e_filt_b2[0]), row2(e_filt_f2[0]),
      *_split_bf16(e_filt_w3[0]), jnp.asarray(_decay_rates()))

    width = DFT_N2 * c
    tw = 4096
    a_data = _left_dft(bf(dft["fd"]), z.reshape(n1, width), out_dtype=BF16, tw=tw, name="dft_data")
    a_filt = _left_dft(bf(dft["ff"]), kfilt.reshape(n1, width), out_dtype=BF16, tw=tw, name="dft_filter")
    slab = pl.BlockSpec((2, 1, DFT_N2, c), lambda i: (0, i, 0, 0))
    twspec = pl.BlockSpec((1, DFT_N2, 1), lambda i: (i, 0, 0))
    bspec = pl.pallas_call(
        _spectral_kernel,
        grid=(n1,),
        in_specs=[slab, slab, twspec, twspec, _full((2 * DFT_N2, 2 * DFT_N2)), _full((2 * DFT_N2, 2 * DFT_N2)),
                  _full((1, c)), _full((1, c))],
        out_specs=slab,
        out_shape=jax.ShapeDtypeStruct((2, n1, DFT_N2, c), BF16),
        compiler_params=_cparams(("parallel",)), name="hyena_spectral",
    )(a_data.reshape(2, n1, DFT_N2, c), a_filt.reshape(2, n1, DFT_N2, c),
      jnp.asarray(dft["twr"]), jnp.asarray(dft["twi"]), bf(dft["fb"]), bf(dft["ib"]), l1, row2(e_hy_d[0]))
    yb = _left_dft(bf(dft["if2"]), bspec.reshape(2 * n1, width), x0g.reshape(n1, width),
                   out_dtype=BF16, tw=tw, name="dft_inverse").reshape(T, c)

    x1 = pl.pallas_call(
        _even_out_kernel, grid=(nt,),
        in_specs=[_rows(tm, D), _rows(tm, A_WIDTH), _rows(tm, c), _full((A_WIDTH + c, D))],
        out_specs=_rows(tm, D), out_shape=jax.ShapeDtypeStruct((T, D), F32),
        compiler_params=_cparams(("parallel",)), name="even_out",
    )(xt, ya, yb, e_w_out[0].astype(BF16))

    in_idx, in_keep, q_idx, q_keep, k_idx, k_keep, v_idx, v_keep = _odd_columns()
    w_o = _gather_cols(o_w_in[0], in_idx, in_keep).astype(BF16)
    w_q = _gather_cols(o_w_qb[0], q_idx, q_keep).astype(BF16)
    w_k = _gather_cols(o_w_kvb[0], k_idx, k_keep).astype(BF16)
    w_v = _gather_cols(o_w_kvb[0], v_idx, v_keep).astype(BF16)
    cs_c, sn_c = _rope_tables_c(L)
    hq = C_HEADS * LANES
    q_c, kt_c, v_c, gc = pl.pallas_call(
        functools.partial(_odd_in_kernel, scale=(C_NOPE + C_ROPE) ** -0.5 * LOG2E),
        grid=(nt,),
        in_specs=[_rows(tm, D), _full((1, D)), _full((D, w_o.shape[1])), _full((1, C_Q_RANK)),
                  _full((1, C_KV_RANK)), _full((C_Q_RANK, hq)), _full((C_KV_RANK, hq)),
                  _full((C_KV_RANK, hq)), table, table],
        out_specs=[_rows(tm, hq), pl.BlockSpec((hq, tm), lambda i: (0, i)), _rows(tm, hq),
                   _rows(tm, C_WIDTH)],
        out_shape=[jax.ShapeDtypeStruct((T, hq), BF16), jax.ShapeDtypeStruct((hq, T), BF16),
                   jax.ShapeDtypeStruct((T, hq), BF16), jax.ShapeDtypeStruct((T, C_WIDTH), F32)],
        compiler_params=_cparams(("parallel",)), name="odd_in",
    )(x1, row2(o_norm[0]), w_o, row2(o_q_a_norm[0]), row2(o_kv_a_norm[0]), w_q, w_k, w_v,
      jnp.asarray(cs_c), jnp.asarray(sn_c))

    yc = _attention(q_c, kt_c, v_c, gc, B=B, L=L, n_pairs=C_HEADS // 2,
                    qa_col=lambda j: 2 * j, qb_col=lambda j: 2 * j + 1,
                    kta_row=lambda j: 2 * j, ktb_row=lambda j: 2 * j + 1,
                    va_col=lambda j: 2 * j, vb_col=lambda j: 2 * j + 1, tq=tq, tk=tk)

    out = pl.pallas_call(
        _odd_out_kernel, grid=(nt,),
        in_specs=[_rows(tm, D), _rows(tm, C_WIDTH), _full((C_WIDTH, D)), _full((1, D))],
        out_specs=_rows(tm, D), out_shape=jax.ShapeDtypeStruct((T, D), F32),
        compiler_params=_cparams(("parallel",)), name="odd_out",
    )(x1, yc, o_w_out[0].astype(BF16), row2(final_norm))
    return out.reshape(B, L, D)
```

```python
import functools
import math

import numpy as np
import jax
import jax.numpy as jnp
from jax import lax
from jax.experimental import pallas as pl
from jax.experimental.pallas import tpu as pltpu

F32 = jnp.float32
BF16 = jnp.bfloat16

D_MODEL = 1024
GRID_W = 64
EPS = 1e-6
ROPE_THETA = 10000.0

A_HEADS, A_KV_HEADS, A_HEAD_DIM = 8, 2, 64
A_WIDTH = A_HEADS * A_HEAD_DIM
HY_C = D_MODEL // 2
HY_EMB_BANDS = 16
HY_EMB_DIM = 1 + 2 * HY_EMB_BANDS
HY_HIDDEN = 64
HY_DECAY_TARGET, HY_FAST_DECAY, HY_SLOW_DECAY = 1e-2, 0.3, 1.5
EVEN_IN = 3328

C_HEADS, C_NOPE, C_ROPE, C_V = 16, 64, 32, 64
C_Q_RANK, C_KV_RANK = 384, 256
C_WIDTH = C_HEADS * C_V

LANES = 128
DFT_N2 = 128
FEAT_W = 64
FILTER_ROWS = 1024
VMEM_LIMIT = 62 * 1024 * 1024
LOG2E = math.log2(math.e)
ATTN_ROW_CHUNK = 32
ATTN_FLAGS = None
ATTN_TQ, ATTN_TK = 1024, 1024
ATTN_SUBTILES = 2
SPECTRAL_SLABS = 4


def _cparams(sem, flags=None):
    return pltpu.CompilerParams(dimension_semantics=sem, vmem_limit_bytes=VMEM_LIMIT, flags=flags)


def _silu(x):
    return x * (1.0 / (1.0 + jnp.exp(-x)))


def _dot(a, b):
    return jnp.dot(a, b, preferred_element_type=F32)


def _split_bf16(a):
    hi = a.astype(BF16)
    return hi, (a - hi.astype(F32)).astype(BF16)


def _dot3(a, w_hi, w_lo):
    a_hi, a_lo = _split_bf16(a)
    return _dot(a_hi, w_hi) + (_dot(a_hi, w_lo) + _dot(a_lo, w_hi))


def _axial_angles(L, dim):
    rows = L // GRID_W
    row = np.repeat(np.arange(rows), GRID_W).astype(np.float64)
    col = np.tile(np.arange(GRID_W), rows).astype(np.float64)
    n = dim // 4
    inv = ROPE_THETA ** (-np.arange(n, dtype=np.float64) / n)
    return np.concatenate([row[:, None] * inv, col[:, None] * inv], axis=-1)


def _rope_tables_a(L):
    ang = _axial_angles(L, A_HEAD_DIM)
    lane = np.arange(LANES)
    j = lane % 32
    sign = np.where(lane // 64 == 0, -1.0, 1.0)
    return (np.cos(ang)[:, j].astype(np.float32), (np.sin(ang)[:, j] * sign).astype(np.float32))


def _rope_tables_c(L):
    ang = _axial_angles(L, C_ROPE)
    cs = np.zeros((L, LANES)); sn = np.zeros((L, LANES))
    cs[:, 0:32] = 1.0; cs[:, 64:96] = 1.0
    cs[:, 32:48] = np.cos(ang); cs[:, 96:112] = np.cos(ang)
    sn[:, 32:48] = -np.sin(ang); sn[:, 96:112] = np.sin(ang)
    return cs.astype(np.float32), sn.astype(np.float32)


def _even_columns():
    q0, k0, v0, rest0 = 0, 512, 640, 768
    idx, keep = [], []
    lane = np.arange(LANES)
    part, hs, j = lane // 64, (lane % 64) // 32, lane % 32
    for pair in range(A_HEADS // 2):
        idx.append(q0 + (2 * pair + hs) * 64 + 32 * part + j); keep.append(np.ones(LANES))
    for g in range(A_KV_HEADS):
        for var in range(2):
            idx.append(k0 + g * 64 + 32 * part + j); keep.append((hs == var).astype(np.float64))
    for g in range(A_KV_HEADS):
        idx.append(v0 + g * 64 + lane % 64); keep.append((lane < 64).astype(np.float64))
    idx.append(np.arange(rest0, EVEN_IN)); keep.append(np.ones(EVEN_IN - rest0))
    return np.concatenate(idx).astype(np.int32), np.concatenate(keep).astype(np.float32)


def _head_gain_index():
    lane = np.arange(LANES)
    return (32 * (lane // 64) + lane % 32).astype(np.int32)


def _segment_mean_matrix():
    lane = np.arange(LANES)
    cls = (lane % 64) // 32
    return ((cls[:, None] == cls[None, :]) / float(A_HEAD_DIM)).astype(np.float32)


def _mla_head_lane_sources():
    src = -np.ones(LANES, np.int64)
    src[0:32] = np.arange(32)
    src[32:48] = 64 + np.arange(16)
    src[64:96] = 32 + np.arange(32)
    src[96:112] = 80 + np.arange(16)
    return src


def _odd_columns():
    src = _mla_head_lane_sources()
    kr = np.where(src >= 64, 640 + (src - 64), 0)
    kr_keep = (src >= 64).astype(np.float32)
    in_idx = np.concatenate([np.arange(0, 640), kr, np.arange(672, 1696)]).astype(np.int32)
    in_keep = np.concatenate([np.ones(640), kr_keep, np.ones(1024)]).astype(np.float32)
    q_idx, q_keep, k_idx, k_keep, v_idx, v_keep = [], [], [], [], [], []
    lane = np.arange(LANES)
    for h in range(C_HEADS):
        q_idx.append(np.where(src >= 0, h * (C_NOPE + C_ROPE) + src, 0)); q_keep.append(src >= 0)
        nope = (src >= 0) & (src < 64)
        k_idx.append(np.where(nope, h * (C_NOPE + C_V) + src, 0)); k_keep.append(nope)
        v_idx.append(h * (C_NOPE + C_V) + C_NOPE + lane % C_V); v_keep.append(lane < C_V)
    cat = lambda xs, dt: np.concatenate(xs).astype(dt)
    return (in_idx, in_keep, cat(q_idx, np.int32), cat(q_keep, np.float32),
            cat(k_idx, np.int32), cat(k_keep, np.float32), cat(v_idx, np.int32), cat(v_keep, np.float32))


def _gather_cols(w, idx, keep):
    w = w.astype(BF16)
    pieces, start, n = [], 0, len(idx)
    while start < n:
        end = start + 1
        while end < n and keep[end] == keep[start] and (keep[start] == 0 or idx[end] == idx[end - 1] + 1):
            end += 1
        if keep[start]:
            pieces.append(w[:, int(idx[start]):int(idx[start]) + end - start])
        else:
            pieces.append(jnp.zeros((w.shape[0], end - start), w.dtype))
        start = end
    return jnp.concatenate(pieces, axis=1)


def _dft_constants(L):
    n1 = 2 * L // DFT_N2
    n = n1 * DFT_N2
    h = n1 // 2
    a1 = -2.0 * np.pi * np.outer(np.arange(n1), np.arange(n1)) / n1
    f1r, f1i = np.cos(a1), np.sin(a1)
    a2 = -2.0 * np.pi * np.outer(np.arange(DFT_N2), np.arange(DFT_N2)) / DFT_N2
    f2r, f2i = np.cos(a2), np.sin(a2)
    at = -2.0 * np.pi * np.outer(np.arange(n1), np.arange(DFT_N2)) / n
    fd = np.block([[f1r[:, :h], -f1i[:, :h]], [f1i[:, :h], f1r[:, :h]]])
    ff = np.concatenate([f1r, f1i], axis=0)
    fb = np.block([[f2r, -f2i], [f2i, f2r]])
    ib = np.block([[f2r, f2i], [-f2i, f2r]])
    if2 = np.block([[f1r[:h], f1i[:h]], [-f1i[:h], f1r[:h]]]) / n
    return dict(n1=n1, fd=fd, ff=ff, fb=fb, ib=ib, if2=if2,
                twr=np.cos(at)[:, :, None].astype(np.float32),
                twi=np.sin(at)[:, :, None].astype(np.float32))


def _filter_features(L):
    r = np.arange(2 * L)
    lag = np.where(r < L, r, 2 * L - r)
    lag = np.where(r == L, 0, lag)
    t = lag / float(L - 1)
    w = 2.0 * np.pi * lag / float(L)
    bands = np.linspace(1e-4, HY_EMB_BANDS - 1, HY_EMB_BANDS)
    feat = np.zeros((2 * L, FEAT_W))
    feat[:, 0] = t
    feat[:, 1:1 + HY_EMB_BANDS] = np.cos(bands[None] * w[:, None])
    feat[:, 1 + HY_EMB_BANDS:HY_EMB_DIM] = -np.sin(bands[None] * w[:, None])
    feat[:, HY_EMB_DIM] = (r != L)
    half = FILTER_ROWS // 2
    tiles = feat.reshape(-1, 2, half, FEAT_W)
    return np.concatenate([tiles[:, 0], tiles[:, 1]], axis=-1).reshape(-1, 2 * FEAT_W).astype(np.float32)


def _block_diag2(w):
    z = jnp.zeros_like(w)
    return jnp.concatenate([jnp.concatenate([w, z], axis=1), jnp.concatenate([z, w], axis=1)], axis=0)


def _decay_rates():
    lo = math.log(HY_DECAY_TARGET) / HY_SLOW_DECAY
    hi = math.log(HY_DECAY_TARGET) / HY_FAST_DECAY
    return np.abs(np.linspace(lo, hi, HY_C)).astype(np.float32)[None]


def _rms_rows(x, gain):
    return x * lax.rsqrt(jnp.mean(x * x, axis=-1, keepdims=True) + EPS) * gain


def _rope(y, cs, sn):
    return y * cs + pltpu.roll(y, 64, 1) * sn


def _with_ones_half(v):
    lane = lax.broadcasted_iota(jnp.int32, v.shape, 1)
    return jnp.where(lane % LANES < 64, v, 1.0)


def _even_in_kernel(x_ref, xprev_ref, xnext_ref, g_ref, w_ref, m_ref, gq_ref, gk_ref, cs_ref, sn_ref,
                    cw_ref, cb_ref, q_ref, kt_ref, v_ref, ga_ref, z_ref, x0g_ref, *, scale, tiles_per_seq):
    tm = x_ref.shape[0]
    x_all = jnp.concatenate([x_ref[...], xprev_ref[...], xnext_ref[...]], axis=0)
    h_all = _rms_rows(x_all, g_ref[...]).astype(BF16)
    h = h_all[0:tm]
    cs, sn, m = cs_ref[...], sn_ref[...], m_ref[...]

    def norm_rope(p, gain):
        ms = _dot((p * p).astype(BF16), m)
        return _rope(p * lax.rsqrt(ms + EPS) * gain, cs, sn)

    pq = _dot(h, w_ref[:, 0:512])
    pk = _dot(h, w_ref[:, 512:1024])
    for b in range(4):
        sl = slice(b * LANES, (b + 1) * LANES)
        q_ref[:, sl] = (norm_rope(pq[:, sl], gq_ref[...]) * scale).astype(BF16)
        kt_ref[sl, :] = norm_rope(pk[:, sl], gk_ref[...]).T.astype(BF16)
    v_ref[...] = _with_ones_half(_dot(h, w_ref[:, 1024:1280])).astype(BF16)
    ga_ref[...] = _silu(_dot(h, w_ref[:, 1280:1792]))
    gb = _silu(_dot(h, w_ref[:, 3328:3840]))

    hy_all = _dot(h_all, w_ref[:, 1792:3328])
    p = hy_all[0:tm]
    i = pl.program_id(0)
    first = (i % tiles_per_seq) == 0
    last = (i % tiles_per_seq) == tiles_per_seq - 1
    halo_prev = jnp.where(first, 0.0, hy_all[tm + 7:tm + 8])
    halo_next = jnp.where(last, 0.0, hy_all[tm + 8:tm + 9])
    row = lax.broadcasted_iota(jnp.int32, p.shape, 0)
    p_prev = jnp.where(row == 0, halo_prev, pltpu.roll(p, 1, 0))
    p_next = jnp.where(row == tm - 1, halo_next, pltpu.roll(p, tm - 1, 0))
    u = p_prev * cw_ref[0:1, :] + p * cw_ref[1:2, :] + p_next * cw_ref[2:3, :] + cb_ref[...]
    c = HY_C
    z_ref[...] = (u[:, 2 * c:3 * c] * u[:, c:2 * c]).astype(BF16)
    x0g_ref[...] = u[:, 0:c] * gb


def _odd_in_kernel(x_ref, g_ref, w_ref, gq_ref, gkv_ref, wq_ref, wk_ref, wv_ref, cs_ref, sn_ref,
                   q_ref, kt_ref, v_ref, gc_ref, *, scale):
    h = _rms_rows(x_ref[...], g_ref[...]).astype(BF16)
    cs, sn = cs_ref[...], sn_ref[...]
    cq = _rms_rows(_dot(h, w_ref[:, 0:384]), gq_ref[...]).astype(BF16)
    ckv = _rms_rows(_dot(h, w_ref[:, 384:640]), gkv_ref[...]).astype(BF16)
    kr = _rope(_dot(h, w_ref[:, 640:768]), cs, sn)
    gc_ref[...] = _silu(_dot(h, w_ref[:, 768:1792]))
    q = _dot(cq, wq_ref[...])
    kn = _dot(ckv, wk_ref[...])
    for b in range(C_HEADS):
        sl = slice(b * LANES, (b + 1) * LANES)
        q_ref[:, sl] = (_rope(q[:, sl], cs, sn) * scale).astype(BF16)
        kt_ref[sl, :] = (kn[:, sl] + kr).T.astype(BF16)
    v_ref[...] = _with_ones_half(_dot(ckv, wv_ref[...])).astype(BF16)


def _attn_kernel(qa_ref, qb_ref, kta_ref, ktb_ref, va_ref, vb_ref, g_ref, o_ref, *scratch, tq, tk):
    nsub = qa_ref.shape[0] // tq
    nk = va_ref.shape[0] // tk
    nb = tk // LANES
    assert nk % 2 == 0
    grid22 = lambda refs: ((refs[0], refs[1]), (refs[2], refs[3]))
    s_scr, p_scr = grid22(scratch[0:4]), grid22(scratch[4:8])
    m_all = (grid22(scratch[8:12]), grid22(scratch[12:16]))
    a_all = (grid22(scratch[16:20]), grid22(scratch[20:24]))
    acc_all = (scratch[24:26], scratch[26:28])
    q_refs = (qa_ref, qb_ref)
    kts = (kta_ref, ktb_ref)
    vs = (va_ref, vb_ref)

    def init(sub):
        for h in range(2):
            m_all[sub % 2][1][h][...] = jnp.full((tq, LANES), -jnp.inf, F32)
            acc_all[sub % 2][h][...] = jnp.zeros((tq, LANES), F32)

    def score_tile(sub, t, c):
        m_scr, a_scr = m_all[sub % 2], a_all[sub % 2]
        off = pl.multiple_of(t * tk, tk)
        for h in range(2):
            s = _dot(q_refs[h][sub * tq:(sub + 1) * tq, :], kts[h][:, pl.ds(off, tk)])
            s_scr[c][h][...] = s
            tile_max = functools.reduce(jnp.maximum, [s[:, j * LANES:(j + 1) * LANES] for j in range(nb)])
            m_prev = m_scr[1 - c][h][...]
            m_new = jnp.maximum(m_prev, jnp.max(tile_max, axis=-1, keepdims=True))
            m_scr[c][h][...] = m_new
            a_scr[c][h][...] = jnp.exp2(m_prev - m_new)

    def consume(sub, t, c):
        m_scr, a_scr, acc_scr = m_all[sub % 2], a_all[sub % 2], acc_all[sub % 2]
        off = pl.multiple_of(t * tk, tk)
        for h in range(2):
            for r in range(0, tq, ATTN_ROW_CHUNK):
                rows = slice(r, r + ATTN_ROW_CHUNK)
                m_new = m_scr[c][h][rows]
                for j in range(nb):
                    cols = slice(j * LANES, (j + 1) * LANES)
                    p_scr[c][h][rows, cols] = jnp.exp2(s_scr[c][h][rows, cols] - m_new).astype(BF16)
            pv = _dot(p_scr[c][h][...], vs[h][pl.ds(off, tk), :])
            acc_scr[h][...] = a_scr[c][h][...] * acc_scr[h][...] + pv

    def finalize(sub):
        rows = slice(sub * tq, (sub + 1) * tq)
        acc = [acc_all[sub % 2][h][...] for h in range(2)]
        out = [a / pltpu.roll(a, 64, 1) for a in acc]
        lane = lax.broadcasted_iota(jnp.int32, (tq, LANES), 1)
        o_ref[rows, :] = (jnp.where(lane < 64, out[0], pltpu.roll(out[1], 64, 1)) * g_ref[rows, :]).astype(o_ref.dtype)

    init(0)
    score_tile(0, 0, 0)
    for sub in range(nsub):

        def body(j, carry, sub=sub):
            t = 2 * j
            score_tile(sub, t + 1, 1)
            consume(sub, t, 0)
            score_tile(sub, t + 2, 0)
            consume(sub, t + 1, 1)
            return carry

        lax.fori_loop(0, nk // 2 - 1, body, 0)
        score_tile(sub, nk - 1, 1)
        consume(sub, nk - 2, 0)
        if sub + 1 < nsub:
            init(sub + 1)
            score_tile(sub + 1, 0, 0)
        consume(sub, nk - 1, 1)
        finalize(sub)


def _filter_kernel(feat_ref, w1h_ref, w1l_ref, b1_ref, f1_ref, w2h_ref, w2l_ref, b2_ref, f2_ref,
                   w3h_ref, w3l_ref, rate_ref, k_ref, l1_ref):
    feat = feat_ref[...]
    half, c = feat.shape[0], rate_ref.shape[1]
    h = jnp.sin(f1_ref[...] * (_dot3(feat, w1h_ref[...], w1l_ref[...]) + b1_ref[...]))
    h = jnp.sin(f2_ref[...] * (_dot3(h, w2h_ref[...], w2l_ref[...]) + b2_ref[...]))
    k2 = _dot3(h, w3h_ref[...], w3l_ref[...])

    @pl.when(pl.program_id(0) == 0)
    def _():
        l1_ref[...] = jnp.zeros_like(l1_ref)

    for part in range(2):
        t = feat[:, part * FEAT_W:part * FEAT_W + 1]
        keep = feat[:, part * FEAT_W + HY_EMB_DIM:part * FEAT_W + HY_EMB_DIM + 1]
        k = k2[:, part * c:(part + 1) * c] * jnp.exp(-t * rate_ref[...]) * keep
        k_ref[part * half:(part + 1) * half, :] = k.astype(BF16)
        l1_ref[...] += jnp.sum(jnp.abs(k), axis=0, keepdims=True)


def _left_dft_kernel(m_ref, x_ref, o_ref):
    o_ref[...] = _dot(m_ref[...].astype(BF16), x_ref[...]).astype(o_ref.dtype)


def _left_dft_gate_kernel(m_ref, x_ref, g_ref, o_ref):
    o_ref[...] = (_dot(m_ref[...].astype(BF16), x_ref[...]) * g_ref[...]).astype(o_ref.dtype)


def _spectral_kernel(ad_ref, af_ref, twr_ref, twi_ref, fb_ref, ib_ref, l1_ref, d_ref, o_ref):
    fb = fb_ref[...].astype(BF16)
    ib = ib_ref[...].astype(BF16)
    inv_l1 = 1.0 / l1_ref[...]
    for s in range(ad_ref.shape[1]):
        twr, twi = twr_ref[s], twi_ref[s]

        def stage2(a_ref):
            ar, ai = a_ref[0, s].astype(F32), a_ref[1, s].astype(F32)
            x = jnp.concatenate([ar * twr - ai * twi, ar * twi + ai * twr], axis=0).astype(BF16)
            xh = _dot(fb, x)
            return xh[:DFT_N2], xh[DFT_N2:]

        xr, xi = stage2(ad_ref)
        kr, ki = stage2(af_ref)
        kr = kr * inv_l1 + d_ref[...]
        ki = ki * inv_l1
        y = jnp.concatenate([xr * kr - xi * ki, xr * ki + xi * kr], axis=0).astype(BF16)
        bt = _dot(ib, y)
        br, bi = bt[:DFT_N2], bt[DFT_N2:]
        o_ref[0, s] = (br * twr + bi * twi).astype(o_ref.dtype)
        o_ref[1, s] = (bi * twr - br * twi).astype(o_ref.dtype)


def _even_out_kernel(x_ref, ya_ref, yb_ref, w_ref, o_ref):
    o_ref[...] = x_ref[...] + _dot(ya_ref[...], w_ref[0:A_WIDTH, :]) + _dot(yb_ref[...], w_ref[A_WIDTH:, :])


def _odd_out_kernel(x_ref, y_ref, w_ref, g_ref, o_ref):
    o_ref[...] = _rms_rows(x_ref[...] + _dot(y_ref[...], w_ref[...]), g_ref[...])


def _full(shape):
    return pl.BlockSpec(shape, lambda *_: (0,) * len(shape))


def _rows(tm, width):
    return pl.BlockSpec((tm, width), lambda i: (i, 0))


def _attention(q, kt, v, gate, *, B, L, n_pairs, qa_col, qb_col, kta_row, ktb_row, va_col, vb_col, tq, tk):
    T = B * L
    rows = tq * ATTN_SUBTILES
    nq = L // rows
    qspec = lambda col: pl.BlockSpec((rows, LANES), lambda b, j, i: (b * nq + i, col(j)))
    ktspec = lambda row: pl.BlockSpec((LANES, L), lambda b, j, i: (row(j), b))
    vspec = lambda col: pl.BlockSpec((L, LANES), lambda b, j, i: (b, col(j)))
    return pl.pallas_call(
        functools.partial(_attn_kernel, tq=tq, tk=tk),
        grid=(B, n_pairs, nq),
        in_specs=[qspec(qa_col), qspec(qb_col), ktspec(kta_row), ktspec(ktb_row),
                  vspec(va_col), vspec(vb_col), qspec(lambda j: j)],
        out_specs=qspec(lambda j: j),
        out_shape=jax.ShapeDtypeStruct((T, n_pairs * LANES), BF16),
        scratch_shapes=([pltpu.VMEM((tq, tk), F32)] * 4 + [pltpu.VMEM((tq, tk), BF16)] * 4
                        + [pltpu.VMEM((tq, LANES), F32)] * 20),
        compiler_params=_cparams(("parallel", "parallel", "arbitrary"), ATTN_FLAGS),
        name="attention",
    )(q, q, kt, kt, v, v, gate)


def _left_dft(mat, x, gate=None, *, out_dtype, tw, name):
    m, k = mat.shape
    width = x.shape[1]
    xspec = pl.BlockSpec((k, tw), lambda i: (0, i))
    ospec = pl.BlockSpec((m, tw), lambda i: (0, i))
    args, specs, body = [mat, x], [_full((m, k)), xspec], _left_dft_kernel
    if gate is not None:
        args.append(gate); specs.append(ospec); body = _left_dft_gate_kernel
    return pl.pallas_call(
        body, grid=(width // tw,), in_specs=specs, out_specs=ospec,
        out_shape=jax.ShapeDtypeStruct((m, width), out_dtype),
        compiler_params=_cparams(("parallel",)), name=name,
    )(*args)


def kernel(x, e_norm, e_w_in, e_q_norm, e_k_norm, e_conv_w, e_conv_b, e_filt_w1, e_filt_b1, e_filt_f1,
           e_filt_w2, e_filt_b2, e_filt_f2, e_filt_w3, e_hy_d, e_w_out,
           o_norm, o_w_in, o_q_a_norm, o_w_qb, o_kv_a_norm, o_w_kvb, o_w_out, final_norm):
    B, L, D = x.shape
    assert D == D_MODEL and B == 2 and L % 1024 == 0 and L % GRID_W == 0
    T = B * L
    tm = 512
    tm_out = 1024
    tq = min(ATTN_TQ, L)
    tk = min(ATTN_TK, L // 2)
    nt = T // tm
    tiles_per_seq = L // tm
    xt = x.reshape(T, D)
    row2 = lambda a: a.reshape(1, -1).astype(F32)

    idx, keep = _even_columns()
    w_e = _gather_cols(e_w_in[0], idx, keep)
    ne = w_e.shape[1]
    cs_a, sn_a = _rope_tables_a(L)
    gidx = _head_gain_index()
    table = pl.BlockSpec((tm, LANES), lambda i: (i % tiles_per_seq, 0))
    c = HY_C
    sub = tm // 8
    q_a, kt_a, v_a, ga, z, x0g = pl.pallas_call(
        functools.partial(_even_in_kernel, scale=A_HEAD_DIM ** -0.5 * LOG2E, tiles_per_seq=tiles_per_seq),
        grid=(nt,),
        in_specs=[_rows(tm, D),
                  pl.BlockSpec((8, D), lambda i: (jnp.maximum(i * sub - 1, 0), 0)),
                  pl.BlockSpec((8, D), lambda i: (jnp.minimum((i + 1) * sub, T // 8 - 1), 0)),
                  _full((1, D)), _full((D, ne)), _full((LANES, LANES)),
                  _full((1, LANES)), _full((1, LANES)), table, table, _full((3, 3 * c)), _full((1, 3 * c))],
        out_specs=[_rows(tm, 512), pl.BlockSpec((512, tm), lambda i: (0, i)), _rows(tm, 256),
                   _rows(tm, 512), _rows(tm, c), _rows(tm, c)],
        out_shape=[jax.ShapeDtypeStruct((T, 512), BF16), jax.ShapeDtypeStruct((512, T), BF16),
                   jax.ShapeDtypeStruct((T, 256), BF16), jax.ShapeDtypeStruct((T, 512), F32),
                   jax.ShapeDtypeStruct((T, c), BF16), jax.ShapeDtypeStruct((T, c), F32)],
        compiler_params=_cparams(("parallel",)), name="even_in",
    )(xt, xt, xt, row2(e_norm[0]), w_e, jnp.asarray(_segment_mean_matrix(), BF16),
      row2(e_q_norm[0][gidx]), row2(e_k_norm[0][gidx]), jnp.asarray(cs_a), jnp.asarray(sn_a),
      e_conv_w[0].astype(F32), row2(e_conv_b[0]))

    ya = _attention(q_a, kt_a, v_a, ga, B=B, L=L, n_pairs=A_HEADS // 2,
                    qa_col=lambda j: j, qb_col=lambda j: j,
                    kta_row=lambda j: 2 * (j // 2), ktb_row=lambda j: 2 * (j // 2) + 1,
                    va_col=lambda j: j // 2, vb_col=lambda j: j // 2, tq=tq, tk=tk)

    dft = _dft_constants(L)
    n1 = dft["n1"]
    bf = lambda a: jnp.asarray(a, F32)
    w1 = jnp.zeros((FEAT_W, HY_HIDDEN), F32).at[:HY_EMB_DIM].set(e_filt_w1[0])
    tf, hid2 = FILTER_ROWS, 2 * HY_HIDDEN
    row2x = lambda a: jnp.tile(row2(a), (1, 2))
    w3_sides = [_block_diag2(e_filt_w3[0][:, side * c:(side + 1) * c]) for side in range(2)]
    w3 = jnp.stack(w3_sides)
    w3spec = pl.BlockSpec((None, hid2, 2 * c), lambda i: (i // (L // tf), 0, 0))
    kfilt, l1 = pl.pallas_call(
        _filter_kernel,
        grid=(2 * L // tf,),
        in_specs=[_rows(tf // 2, 2 * FEAT_W)] + [_full((2 * FEAT_W, hid2))] * 2 + [_full((1, hid2))] * 2
                 + [_full((hid2, hid2))] * 2 + [_full((1, hid2))] * 2 + [w3spec] * 2 + [_full((1, c))],
        out_specs=[_rows(tf, c), _full((1, c))],
        out_shape=[jax.ShapeDtypeStruct((2 * L, c), BF16), jax.ShapeDtypeStruct((1, c), F32)],
        compiler_params=_cparams(("arbitrary",)), name="hyena_filter",
    )(jnp.asarray(_filter_features(L)), *_split_bf16(_block_diag2(w1)), row2x(e_filt_b1[0]), row2x(e_filt_f1[0]),
      *_split_bf16(_block_diag2(e_filt_w2[0])), row2x(e_filt_b2[0]), row2x(e_filt_f2[0]),
      *_split_bf16(w3), jnp.asarray(_decay_rates()))

    width = DFT_N2 * c
    tw = 4096
    a_data = _left_dft(bf(dft["fd"]), z.reshape(n1, width), out_dtype=BF16, tw=tw, name="dft_data")
    a_filt = _left_dft(bf(dft["ff"]), kfilt.reshape(n1, width), out_dtype=BF16, tw=tw, name="dft_filter")
    slab = pl.BlockSpec((2, SPECTRAL_SLABS, DFT_N2, c), lambda i: (0, i, 0, 0))
    twspec = pl.BlockSpec((SPECTRAL_SLABS, DFT_N2, 1), lambda i: (i, 0, 0))
    bspec = pl.pallas_call(
        _spectral_kernel,
        grid=(n1 // SPECTRAL_SLABS,),
        in_specs=[slab, slab, twspec, twspec, _full((2 * DFT_N2, 2 * DFT_N2)), _full((2 * DFT_N2, 2 * DFT_N2)),
                  _full((1, c)), _full((1, c))],
        out_specs=slab,
        out_shape=jax.ShapeDtypeStruct((2, n1, DFT_N2, c), BF16),
        compiler_params=_cparams(("parallel",)), name="hyena_spectral",
    )(a_data.reshape(2, n1, DFT_N2, c), a_filt.reshape(2, n1, DFT_N2, c),
      jnp.asarray(dft["twr"]), jnp.asarray(dft["twi"]), bf(dft["fb"]), bf(dft["ib"]), l1, row2(e_hy_d[0]))
    yb = _left_dft(bf(dft["if2"]), bspec.reshape(2 * n1, width), x0g.reshape(n1, width),
                   out_dtype=BF16, tw=tw, name="dft_inverse").reshape(T, c)

    x1 = pl.pallas_call(
        _even_out_kernel, grid=(T // tm_out,),
        in_specs=[_rows(tm_out, D), _rows(tm_out, A_WIDTH), _rows(tm_out, c), _full((A_WIDTH + c, D))],
        out_specs=_rows(tm_out, D), out_shape=jax.ShapeDtypeStruct((T, D), F32),
        compiler_params=_cparams(("parallel",)), name="even_out",
    )(xt, ya, yb, e_w_out[0].astype(BF16))

    in_idx, in_keep, q_idx, q_keep, k_idx, k_keep, v_idx, v_keep = _odd_columns()
    w_o = _gather_cols(o_w_in[0], in_idx, in_keep)
    w_q = _gather_cols(o_w_qb[0], q_idx, q_keep)
    w_k = _gather_cols(o_w_kvb[0], k_idx, k_keep)
    w_v = _gather_cols(o_w_kvb[0], v_idx, v_keep)
    cs_c, sn_c = _rope_tables_c(L)
    hq = C_HEADS * LANES
    q_c, kt_c, v_c, gc = pl.pallas_call(
        functools.partial(_odd_in_kernel, scale=(C_NOPE + C_ROPE) ** -0.5 * LOG2E),
        grid=(nt,),
        in_specs=[_rows(tm, D), _full((1, D)), _full((D, w_o.shape[1])), _full((1, C_Q_RANK)),
                  _full((1, C_KV_RANK)), _full((C_Q_RANK, hq)), _full((C_KV_RANK, hq)),
                  _full((C_KV_RANK, hq)), table, table],
        out_specs=[_rows(tm, hq), pl.BlockSpec((hq, tm), lambda i: (0, i)), _rows(tm, hq),
                   _rows(tm, C_WIDTH)],
        out_shape=[jax.ShapeDtypeStruct((T, hq), BF16), jax.ShapeDtypeStruct((hq, T), BF16),
                   jax.ShapeDtypeStruct((T, hq), BF16), jax.ShapeDtypeStruct((T, C_WIDTH), F32)],
        compiler_params=_cparams(("parallel",)), name="odd_in",
    )(x1, row2(o_norm[0]), w_o, row2(o_q_a_norm[0]), row2(o_kv_a_norm[0]), w_q, w_k, w_v,
      jnp.asarray(cs_c), jnp.asarray(sn_c))

    yc = _attention(q_c, kt_c, v_c, gc, B=B, L=L, n_pairs=C_HEADS // 2,
                    qa_col=lambda j: 2 * j, qb_col=lambda j: 2 * j + 1,
                    kta_row=lambda j: 2 * j, ktb_row=lambda j: 2 * j + 1,
                    va_col=lambda j: 2 * j, vb_col=lambda j: 2 * j + 1, tq=tq, tk=tk)

    out = pl.pallas_call(
        _odd_out_kernel, grid=(T // tm_out,),
        in_specs=[_rows(tm_out, D), _rows(tm_out, C_WIDTH), _full((C_WIDTH, D)), _full((1, D))],
        out_specs=_rows(tm_out, D), out_shape=jax.ShapeDtypeStruct((T, D), F32),
        compiler_params=_cparams(("parallel",)), name="odd_out",
    )(x1, yc, o_w_out[0].astype(BF16), row2(final_norm))
    return out.reshape(B, L, D)
```

```python
import functools
import math

import numpy as np
import jax
import jax.numpy as jnp
from jax import lax
from jax.experimental import pallas as pl
from jax.experimental.pallas import tpu as pltpu

F32 = jnp.float32
BF16 = jnp.bfloat16

D_MODEL = 1024
GRID_W = 64
EPS = 1e-6
ROPE_THETA = 10000.0

A_HEADS, A_KV_HEADS, A_HEAD_DIM = 8, 2, 64
A_WIDTH = A_HEADS * A_HEAD_DIM
HY_C = D_MODEL // 2
HY_EMB_BANDS = 16
HY_EMB_DIM = 1 + 2 * HY_EMB_BANDS
HY_HIDDEN = 64
HY_DECAY_TARGET, HY_FAST_DECAY, HY_SLOW_DECAY = 1e-2, 0.3, 1.5
EVEN_IN = 3328

C_HEADS, C_NOPE, C_ROPE, C_V = 16, 64, 32, 64
C_Q_RANK, C_KV_RANK = 384, 256
C_WIDTH = C_HEADS * C_V

LANES = 128
DFT_N2 = 128
FEAT_W = 64
FILTER_ROWS = 1024
DFT_COLS = 16
ROW_TILE = 1024
VMEM_LIMIT = 62 * 1024 * 1024
LOG2E = math.log2(math.e)
ATTN_ROW_CHUNK = 32
ATTN_TQ, ATTN_TK = 512, 2048
ATTN_SUBTILES = 2
SPECTRAL_SLABS = 8


def _cparams(sem):
    return pltpu.CompilerParams(dimension_semantics=sem, vmem_limit_bytes=VMEM_LIMIT)


def _silu(x):
    return x * (1.0 / (1.0 + jnp.exp(-x)))


def _dot(a, b):
    return jnp.dot(a, b, preferred_element_type=F32)


def _split_bf16(a):
    hi = a.astype(BF16)
    return hi, (a - hi.astype(F32)).astype(BF16)


def _dot3(a, w_hi, w_lo):
    a_hi, a_lo = _split_bf16(a)
    return _dot(a_hi, w_hi) + (_dot(a_hi, w_lo) + _dot(a_lo, w_hi))


def _axial_angles(L, dim):
    rows = L // GRID_W
    row = np.repeat(np.arange(rows), GRID_W).astype(np.float64)
    col = np.tile(np.arange(GRID_W), rows).astype(np.float64)
    n = dim // 4
    inv = ROPE_THETA ** (-np.arange(n, dtype=np.float64) / n)
    return np.concatenate([row[:, None] * inv, col[:, None] * inv], axis=-1)


def _rope_tables_a(L):
    ang = _axial_angles(L, A_HEAD_DIM)
    lane = np.arange(LANES)
    j = lane % 32
    sign = np.where(lane // 64 == 0, -1.0, 1.0)
    return (np.cos(ang)[:, j].astype(np.float32), (np.sin(ang)[:, j] * sign).astype(np.float32))


def _rope_tables_c(L):
    ang = _axial_angles(L, C_ROPE)
    cs = np.zeros((L, LANES)); sn = np.zeros((L, LANES))
    cs[:, 0:32] = 1.0; cs[:, 64:96] = 1.0
    cs[:, 32:48] = np.cos(ang); cs[:, 96:112] = np.cos(ang)
    sn[:, 32:48] = -np.sin(ang); sn[:, 96:112] = np.sin(ang)
    return cs.astype(np.float32), sn.astype(np.float32)


def _even_columns():
    q0, k0, v0, rest0 = 0, 512, 640, 768
    idx, keep = [], []
    lane = np.arange(LANES)
    part, hs, j = lane // 64, (lane % 64) // 32, lane % 32
    for pair in range(A_HEADS // 2):
        idx.append(q0 + (2 * pair + hs) * 64 + 32 * part + j); keep.append(np.ones(LANES))
    for g in range(A_KV_HEADS):
        for var in range(2):
            idx.append(k0 + g * 64 + 32 * part + j); keep.append((hs == var).astype(np.float64))
    for g in range(A_KV_HEADS):
        idx.append(v0 + g * 64 + lane % 64); keep.append((lane < 64).astype(np.float64))
    idx.append(np.arange(rest0, EVEN_IN)); keep.append(np.ones(EVEN_IN - rest0))
    return np.concatenate(idx).astype(np.int32), np.concatenate(keep).astype(np.float32)


def _head_gain_index():
    lane = np.arange(LANES)
    return (32 * (lane // 64) + lane % 32).astype(np.int32)


def _segment_mean_matrix():
    lane = np.arange(LANES)
    cls = (lane % 64) // 32
    return ((cls[:, None] == cls[None, :]) / float(A_HEAD_DIM)).astype(np.float32)


def _mla_head_lane_sources():
    src = -np.ones(LANES, np.int64)
    src[0:32] = np.arange(32)
    src[32:48] = 64 + np.arange(16)
    src[64:96] = 32 + np.arange(32)
    src[96:112] = 80 + np.arange(16)
    return src


def _odd_columns():
    src = _mla_head_lane_sources()
    kr = np.where(src >= 64, 640 + (src - 64), 0)
    kr_keep = (src >= 64).astype(np.float32)
    in_idx = np.concatenate([np.arange(0, 640), kr, np.arange(672, 1696)]).astype(np.int32)
    in_keep = np.concatenate([np.ones(640), kr_keep, np.ones(1024)]).astype(np.float32)
    q_idx, q_keep, k_idx, k_keep, v_idx, v_keep = [], [], [], [], [], []
    lane = np.arange(LANES)
    for h in range(C_HEADS):
        q_idx.append(np.where(src >= 0, h * (C_NOPE + C_ROPE) + src, 0)); q_keep.append(src >= 0)
        nope = (src >= 0) & (src < 64)
        k_idx.append(np.where(nope, h * (C_NOPE + C_V) + src, 0)); k_keep.append(nope)
        v_idx.append(h * (C_NOPE + C_V) + C_NOPE + lane % C_V); v_keep.append(lane < C_V)
    cat = lambda xs, dt: np.concatenate(xs).astype(dt)
    return (in_idx, in_keep, cat(q_idx, np.int32), cat(q_keep, np.float32),
            cat(k_idx, np.int32), cat(k_keep, np.float32), cat(v_idx, np.int32), cat(v_keep, np.float32))


def _gather_cols(w, idx, keep):
    w = w.astype(BF16)
    pieces, start, n = [], 0, len(idx)
    while start < n:
        end = start + 1
        while end < n and keep[end] == keep[start] and (keep[start] == 0 or idx[end] == idx[end - 1] + 1):
            end += 1
        if keep[start]:
            pieces.append(w[:, int(idx[start]):int(idx[start]) + end - start])
        else:
            pieces.append(jnp.zeros((w.shape[0], end - start), w.dtype))
        start = end
    return jnp.concatenate(pieces, axis=1)


def _dft_constants(L):
    n1 = 2 * L // DFT_N2
    n = n1 * DFT_N2
    h = n1 // 2
    a1 = -2.0 * np.pi * np.outer(np.arange(n1), np.arange(n1)) / n1
    f1r, f1i = np.cos(a1), np.sin(a1)
    a2 = -2.0 * np.pi * np.outer(np.arange(DFT_N2), np.arange(DFT_N2)) / DFT_N2
    f2r, f2i = np.cos(a2), np.sin(a2)
    at = -2.0 * np.pi * np.outer(np.arange(n1), np.arange(DFT_N2)) / n
    fd = np.block([[f1r[:, :h], -f1i[:, :h]], [f1i[:, :h], f1r[:, :h]]])
    ff = np.concatenate([f1r, f1i], axis=0)
    fb = np.block([[f2r, -f2i], [f2i, f2r]])
    ib = np.block([[f2r, f2i], [-f2i, f2r]])
    if2 = np.block([[f1r[:h], f1i[:h]], [-f1i[:h], f1r[:h]]]) / n
    return dict(n1=n1, fd=fd, ff=ff, fb=fb, ib=ib, if2=if2,
                twr=np.cos(at)[:, :, None].astype(np.float32),
                twi=np.sin(at)[:, :, None].astype(np.float32))


def _filter_features(L):
    r = np.arange(2 * L)
    lag = np.where(r < L, r, 2 * L - r)
    lag = np.where(r == L, 0, lag)
    t = lag / float(L - 1)
    w = 2.0 * np.pi * lag / float(L)
    bands = np.linspace(1e-4, HY_EMB_BANDS - 1, HY_EMB_BANDS)
    feat = np.zeros((2 * L, FEAT_W))
    feat[:, 0] = t
    feat[:, 1:1 + HY_EMB_BANDS] = np.cos(bands[None] * w[:, None])
    feat[:, 1 + HY_EMB_BANDS:HY_EMB_DIM] = -np.sin(bands[None] * w[:, None])
    feat[:, HY_EMB_DIM] = (r != L)
    half = FILTER_ROWS // 2
    tiles = feat.reshape(-1, 2, half, FEAT_W)
    return np.concatenate([tiles[:, 0], tiles[:, 1]], axis=-1).reshape(-1, 2 * FEAT_W).astype(np.float32)


def _block_diag2(w):
    z = jnp.zeros_like(w)
    return jnp.concatenate([jnp.concatenate([w, z], axis=1), jnp.concatenate([z, w], axis=1)], axis=0)


def _decay_rates():
    lo = math.log(HY_DECAY_TARGET) / HY_SLOW_DECAY
    hi = math.log(HY_DECAY_TARGET) / HY_FAST_DECAY
    return np.abs(np.linspace(lo, hi, HY_C)).astype(np.float32)[None]


def _rms_rows(x, gain):
    return x * lax.rsqrt(jnp.mean(x * x, axis=-1, keepdims=True) + EPS) * gain


def _rope(y, cs, sn):
    return y * cs + pltpu.roll(y, 64, 1) * sn


def _with_ones_half(v):
    lane = lax.broadcasted_iota(jnp.int32, v.shape, 1)
    return jnp.where(lane % LANES < 64, v, 1.0)


def _even_in_kernel(x_ref, xprev_ref, xnext_ref, g_ref, w_ref, m_ref, gq_ref, gk_ref, cs_ref, sn_ref,
                    cw_ref, cb_ref, q_ref, kt_ref, v_ref, ga_ref, z_ref, x0g_ref, *, scale, tiles_per_seq):
    tm = x_ref.shape[0]
    x_all = jnp.concatenate([x_ref[...], xprev_ref[...], xnext_ref[...]], axis=0)
    h_all = _rms_rows(x_all, g_ref[...]).astype(BF16)
    h = h_all[0:tm]
    cs, sn, m = cs_ref[...], sn_ref[...], m_ref[...]

    def norm_rope(p, gain):
        ms = _dot((p * p).astype(BF16), m)
        return _rope(p * lax.rsqrt(ms + EPS) * gain, cs, sn)

    pq = _dot(h, w_ref[:, 0:512])
    pk = _dot(h, w_ref[:, 512:1024])
    for b in range(4):
        sl = slice(b * LANES, (b + 1) * LANES)
        q_ref[:, sl] = (norm_rope(pq[:, sl], gq_ref[...]) * scale).astype(BF16)
        kt_ref[sl, :] = norm_rope(pk[:, sl], gk_ref[...]).T.astype(BF16)
    v_ref[...] = _with_ones_half(_dot(h, w_ref[:, 1024:1280])).astype(BF16)
    ga_ref[...] = _silu(_dot(h, w_ref[:, 1280:1792]))
    gb = _silu(_dot(h, w_ref[:, 3328:3840]))

    hy_all = _dot(h_all, w_ref[:, 1792:3328])
    p = hy_all[0:tm]
    i = pl.program_id(0)
    first = (i % tiles_per_seq) == 0
    last = (i % tiles_per_seq) == tiles_per_seq - 1
    halo_prev = jnp.where(first, 0.0, hy_all[tm + 7:tm + 8])
    halo_next = jnp.where(last, 0.0, hy_all[tm + 8:tm + 9])
    row = lax.broadcasted_iota(jnp.int32, p.shape, 0)
    p_prev = jnp.where(row == 0, halo_prev, pltpu.roll(p, 1, 0))
    p_next = jnp.where(row == tm - 1, halo_next, pltpu.roll(p, tm - 1, 0))
    u = p_prev * cw_ref[0:1, :] + p * cw_ref[1:2, :] + p_next * cw_ref[2:3, :] + cb_ref[...]
    c = HY_C
    z_ref[...] = (u[:, 2 * c:3 * c] * u[:, c:2 * c]).astype(BF16)
    x0g_ref[...] = u[:, 0:c] * gb


def _odd_in_kernel(x_ref, g_ref, w_ref, gq_ref, gkv_ref, wq_ref, wk_ref, wv_ref, cs_ref, sn_ref,
                   q_ref, kt_ref, v_ref, gc_ref, *, scale):
    h = _rms_rows(x_ref[...], g_ref[...]).astype(BF16)
    cs, sn = cs_ref[...], sn_ref[...]
    cq = _rms_rows(_dot(h, w_ref[:, 0:384]), gq_ref[...]).astype(BF16)
    ckv = _rms_rows(_dot(h, w_ref[:, 384:640]), gkv_ref[...]).astype(BF16)
    kr = _rope(_dot(h, w_ref[:, 640:768]), cs, sn)
    gc_ref[...] = _silu(_dot(h, w_ref[:, 768:1792]))
    q = _dot(cq, wq_ref[...])
    kn = _dot(ckv, wk_ref[...])
    for b in range(C_HEADS):
        sl = slice(b * LANES, (b + 1) * LANES)
        q_ref[:, sl] = (_rope(q[:, sl], cs, sn) * scale).astype(BF16)
        kt_ref[sl, :] = (kn[:, sl] + kr).T.astype(BF16)
    v_ref[...] = _with_ones_half(_dot(ckv, wv_ref[...])).astype(BF16)


def _attn_kernel(qa_ref, qb_ref, kta_ref, ktb_ref, va_ref, vb_ref, g_ref, o_ref, *scratch, tq, tk):
    nsub = qa_ref.shape[0] // tq
    nk = va_ref.shape[0] // tk
    nb = tk // LANES
    assert nk % 2 == 0
    grid22 = lambda refs: ((refs[0], refs[1]), (refs[2], refs[3]))
    s_scr, p_scr = grid22(scratch[0:4]), grid22(scratch[4:8])
    m_all = (grid22(scratch[8:12]), grid22(scratch[12:16]))
    a_all = (grid22(scratch[16:20]), grid22(scratch[20:24]))
    acc_all = (scratch[24:26], scratch[26:28])
    q_refs = (qa_ref, qb_ref)
    kts = (kta_ref, ktb_ref)
    vs = (va_ref, vb_ref)

    def init(sub):
        for h in range(2):
            m_all[sub % 2][1][h][...] = jnp.full((tq, LANES), -jnp.inf, F32)
            acc_all[sub % 2][h][...] = jnp.zeros((tq, LANES), F32)

    def score_tile(sub, t, c):
        m_scr, a_scr = m_all[sub % 2], a_all[sub % 2]
        off = pl.multiple_of(t * tk, tk)
        for h in range(2):
            s = _dot(q_refs[h][sub * tq:(sub + 1) * tq, :], kts[h][:, pl.ds(off, tk)])
            s_scr[c][h][...] = s
            tile_max = functools.reduce(jnp.maximum, [s[:, j * LANES:(j + 1) * LANES] for j in range(nb)])
            m_prev = m_scr[1 - c][h][...]
            m_new = jnp.maximum(m_prev, jnp.max(tile_max, axis=-1, keepdims=True))
            m_scr[c][h][...] = m_new
            a_scr[c][h][...] = jnp.exp2(m_prev - m_new)

    def consume(sub, t, c):
        m_scr, a_scr, acc_scr = m_all[sub % 2], a_all[sub % 2], acc_all[sub % 2]
        off = pl.multiple_of(t * tk, tk)
        for h in range(2):
            for r in range(0, tq, ATTN_ROW_CHUNK):
                rows = slice(r, r + ATTN_ROW_CHUNK)
                m_new = m_scr[c][h][rows]
                for j in range(nb):
                    cols = slice(j * LANES, (j + 1) * LANES)
                    p_scr[c][h][rows, cols] = jnp.exp2(s_scr[c][h][rows, cols] - m_new).astype(BF16)
            pv = _dot(p_scr[c][h][...], vs[h][pl.ds(off, tk), :])
            acc_scr[h][...] = a_scr[c][h][...] * acc_scr[h][...] + pv

    def finalize(sub):
        rows = slice(sub * tq, (sub + 1) * tq)
        acc = [acc_all[sub % 2][h][...] for h in range(2)]
        out = [a / pltpu.roll(a, 64, 1) for a in acc]
        lane = lax.broadcasted_iota(jnp.int32, (tq, LANES), 1)
        o_ref[rows, :] = (jnp.where(lane < 64, out[0], pltpu.roll(out[1], 64, 1)) * g_ref[rows, :]).astype(o_ref.dtype)

    init(0)
    score_tile(0, 0, 0)
    for sub in range(nsub):

        def body(j, carry, sub=sub):
            t = 2 * j
            score_tile(sub, t + 1, 1)
            consume(sub, t, 0)
            score_tile(sub, t + 2, 0)
            consume(sub, t + 1, 1)
            return carry

        lax.fori_loop(0, nk // 2 - 1, body, 0)
        score_tile(sub, nk - 1, 1)
        consume(sub, nk - 2, 0)
        if sub + 1 < nsub:
            init(sub + 1)
            score_tile(sub + 1, 0, 0)
        consume(sub, nk - 1, 1)
        finalize(sub)


def _filter_kernel(feat_ref, w1h_ref, w1l_ref, b1_ref, f1_ref, w2h_ref, w2l_ref, b2_ref, f2_ref,
                   w3h_ref, w3l_ref, rate_ref, k_ref, l1_ref):
    feat = feat_ref[...]
    half, c = feat.shape[0], rate_ref.shape[1]
    h = jnp.sin(f1_ref[...] * (_dot3(feat, w1h_ref[...], w1l_ref[...]) + b1_ref[...]))
    h = jnp.sin(f2_ref[...] * (_dot3(h, w2h_ref[...], w2l_ref[...]) + b2_ref[...]))
    k2 = _dot3(h, w3h_ref[...], w3l_ref[...])

    @pl.when(pl.program_id(0) == 0)
    def _():
        l1_ref[...] = jnp.zeros_like(l1_ref)

    for part in range(2):
        t = feat[:, part * FEAT_W:part * FEAT_W + 1]
        keep = feat[:, part * FEAT_W + HY_EMB_DIM:part * FEAT_W + HY_EMB_DIM + 1]
        k = k2[:, part * c:(part + 1) * c] * jnp.exp(-t * rate_ref[...]) * keep
        k_ref[part * half:(part + 1) * half, :] = k.astype(BF16)
        l1_ref[...] += jnp.sum(jnp.abs(k), axis=0, keepdims=True)


def _dft_first_kernel(m_ref, x_ref, o_ref):
    m = m_ref[...].astype(BF16)
    c = x_ref.shape[2]
    for j in range(x_ref.shape[1]):
        o_ref[:, j * c:(j + 1) * c] = _dot(m, x_ref[:, j, :]).astype(o_ref.dtype)


def _dft_last_kernel(m_ref, x_ref, g_ref, o_ref):
    m = m_ref[...].astype(BF16)
    c = g_ref.shape[2]
    for j in range(g_ref.shape[1]):
        o_ref[:, j, :] = (_dot(m, x_ref[:, j * c:(j + 1) * c]) * g_ref[:, j, :]).astype(o_ref.dtype)


def _spectral_kernel(ad_ref, af_ref, twr_ref, twi_ref, fb_ref, ib_ref, l1_ref, d_ref, o_ref):
    fb = fb_ref[...].astype(BF16)
    ib = ib_ref[...].astype(BF16)
    inv_l1 = 1.0 / l1_ref[...]
    for s in range(ad_ref.shape[1]):
        twr, twi = twr_ref[s], twi_ref[s]

        def stage2(a_ref):
            ar, ai = a_ref[0, s].astype(F32), a_ref[1, s].astype(F32)
            x = jnp.concatenate([ar * twr - ai * twi, ar * twi + ai * twr], axis=0).astype(BF16)
            xh = _dot(fb, x)
            return xh[:DFT_N2], xh[DFT_N2:]

        xr, xi = stage2(ad_ref)
        kr, ki = stage2(af_ref)
        kr = kr * inv_l1 + d_ref[...]
        ki = ki * inv_l1
        y = jnp.concatenate([xr * kr - xi * ki, xr * ki + xi * kr], axis=0).astype(BF16)
        bt = _dot(ib, y)
        br, bi = bt[:DFT_N2], bt[DFT_N2:]
        o_ref[0, s] = (br * twr + bi * twi).astype(o_ref.dtype)
        o_ref[1, s] = (bi * twr - br * twi).astype(o_ref.dtype)


def _even_out_kernel(x_ref, ya_ref, yb_ref, w_ref, o_ref):
    o_ref[...] = x_ref[...] + _dot(ya_ref[...], w_ref[0:A_WIDTH, :]) + _dot(yb_ref[...], w_ref[A_WIDTH:, :])


def _odd_out_kernel(x_ref, y_ref, w_ref, g_ref, o_ref):
    o_ref[...] = _rms_rows(x_ref[...] + _dot(y_ref[...], w_ref[...]), g_ref[...])


def _full(shape):
    return pl.BlockSpec(shape, lambda *_: (0,) * len(shape))


def _rows(tm, width):
    return pl.BlockSpec((tm, width), lambda i: (i, 0))


def _attention(q, kt, v, gate, *, B, L, n_pairs, qa_col, qb_col, kta_row, ktb_row, va_col, vb_col, tq, tk):
    T = B * L
    rows = tq * ATTN_SUBTILES
    nq = L // rows
    qspec = lambda col: pl.BlockSpec((rows, LANES), lambda b, j, i: (b * nq + i, col(j)))
    ktspec = lambda row: pl.BlockSpec((LANES, L), lambda b, j, i: (row(j), b))
    vspec = lambda col: pl.BlockSpec((L, LANES), lambda b, j, i: (b, col(j)))
    return pl.pallas_call(
        functools.partial(_attn_kernel, tq=tq, tk=tk),
        grid=(B, n_pairs, nq),
        in_specs=[qspec(qa_col), qspec(qb_col), ktspec(kta_row), ktspec(ktb_row),
                  vspec(va_col), vspec(vb_col), qspec(lambda j: j)],
        out_specs=qspec(lambda j: j),
        out_shape=jax.ShapeDtypeStruct((T, n_pairs * LANES), BF16),
        scratch_shapes=([pltpu.VMEM((tq, tk), F32)] * 4 + [pltpu.VMEM((tq, tk), BF16)] * 4
                        + [pltpu.VMEM((tq, LANES), F32)] * 20),
        compiler_params=_cparams(("parallel", "parallel", "arbitrary")),
        name="attention",
    )(q, q, kt, kt, v, v, gate)


def _dft_first(mat, x3, *, name):
    m, k = mat.shape
    _, n2, c = x3.shape
    w = DFT_COLS
    return pl.pallas_call(
        _dft_first_kernel, grid=(n2 // w,),
        in_specs=[_full((m, k)), pl.BlockSpec((k, w, c), lambda i: (0, i, 0))],
        out_specs=pl.BlockSpec((m, w * c), lambda i: (0, i)),
        out_shape=jax.ShapeDtypeStruct((m, n2 * c), BF16),
        compiler_params=_cparams(("parallel",)), name=name,
    )(mat, x3)


def _dft_last(mat, x, gate3, *, name):
    m, k = mat.shape
    _, n2, c = gate3.shape
    w = DFT_COLS
    tspec = pl.BlockSpec((m, w, c), lambda i: (0, i, 0))
    return pl.pallas_call(
        _dft_last_kernel, grid=(n2 // w,),
        in_specs=[_full((m, k)), pl.BlockSpec((k, w * c), lambda i: (0, i)), tspec],
        out_specs=tspec,
        out_shape=jax.ShapeDtypeStruct((m, n2, c), BF16),
        compiler_params=_cparams(("parallel",)), name=name,
    )(mat, x, gate3)


def kernel(x, e_norm, e_w_in, e_q_norm, e_k_norm, e_conv_w, e_conv_b, e_filt_w1, e_filt_b1, e_filt_f1,
           e_filt_w2, e_filt_b2, e_filt_f2, e_filt_w3, e_hy_d, e_w_out,
           o_norm, o_w_in, o_q_a_norm, o_w_qb, o_kv_a_norm, o_w_kvb, o_w_out, final_norm):
    B, L, D = x.shape
    assert D == D_MODEL and B == 2 and L % 1024 == 0 and L % GRID_W == 0
    T = B * L
    tm = min(ROW_TILE, L)
    tq = min(ATTN_TQ, L)
    tk = min(ATTN_TK, L // 2)
    nt = T // tm
    tiles_per_seq = L // tm
    xt = x.reshape(T, D)
    row2 = lambda a: a.reshape(1, -1).astype(F32)

    idx, keep = _even_columns()
    w_e = _gather_cols(e_w_in[0], idx, keep)
    ne = w_e.shape[1]
    cs_a, sn_a = _rope_tables_a(L)
    gidx = _head_gain_index()
    table = pl.BlockSpec((tm, LANES), lambda i: (i % tiles_per_seq, 0))
    c = HY_C
    sub = tm // 8
    q_a, kt_a, v_a, ga, z, x0g = pl.pallas_call(
        functools.partial(_even_in_kernel, scale=A_HEAD_DIM ** -0.5 * LOG2E, tiles_per_seq=tiles_per_seq),
        grid=(nt,),
        in_specs=[_rows(tm, D),
                  pl.BlockSpec((8, D), lambda i: (jnp.maximum(i * sub - 1, 0), 0)),
                  pl.BlockSpec((8, D), lambda i: (jnp.minimum((i + 1) * sub, T // 8 - 1), 0)),
                  _full((1, D)), _full((D, ne)), _full((LANES, LANES)),
                  _full((1, LANES)), _full((1, LANES)), table, table, _full((3, 3 * c)), _full((1, 3 * c))],
        out_specs=[_rows(tm, 512), pl.BlockSpec((512, tm), lambda i: (0, i)), _rows(tm, 256),
                   _rows(tm, 512), _rows(tm, c), _rows(tm, c)],
        out_shape=[jax.ShapeDtypeStruct((T, 512), BF16), jax.ShapeDtypeStruct((512, T), BF16),
                   jax.ShapeDtypeStruct((T, 256), BF16), jax.ShapeDtypeStruct((T, 512), F32),
                   jax.ShapeDtypeStruct((T, c), BF16), jax.ShapeDtypeStruct((T, c), F32)],
        compiler_params=_cparams(("parallel",)), name="even_in",
    )(xt, xt, xt, row2(e_norm[0]), w_e, jnp.asarray(_segment_mean_matrix(), BF16),
      row2(e_q_norm[0][gidx]), row2(e_k_norm[0][gidx]), jnp.asarray(cs_a), jnp.asarray(sn_a),
      e_conv_w[0].astype(F32), row2(e_conv_b[0]))

    ya = _attention(q_a, kt_a, v_a, ga, B=B, L=L, n_pairs=A_HEADS // 2,
                    qa_col=lambda j: j, qb_col=lambda j: j,
                    kta_row=lambda j: 2 * (j // 2), ktb_row=lambda j: 2 * (j // 2) + 1,
                    va_col=lambda j: j // 2, vb_col=lambda j: j // 2, tq=tq, tk=tk)

    dft = _dft_constants(L)
    n1 = dft["n1"]
    f32c = lambda a: jnp.asarray(a, F32)
    w1 = jnp.zeros((FEAT_W, HY_HIDDEN), F32).at[:HY_EMB_DIM].set(e_filt_w1[0])
    tf, hid2 = FILTER_ROWS, 2 * HY_HIDDEN
    row2x = lambda a: jnp.tile(row2(a), (1, 2))
    w3_sides = [_block_diag2(e_filt_w3[0][:, side * c:(side + 1) * c]) for side in range(2)]
    w3 = jnp.stack(w3_sides)
    w3spec = pl.BlockSpec((None, hid2, 2 * c), lambda i: (i // (L // tf), 0, 0))
    kfilt, l1 = pl.pallas_call(
        _filter_kernel,
        grid=(2 * L // tf,),
        in_specs=[_rows(tf // 2, 2 * FEAT_W)] + [_full((2 * FEAT_W, hid2))] * 2 + [_full((1, hid2))] * 2
                 + [_full((hid2, hid2))] * 2 + [_full((1, hid2))] * 2 + [w3spec] * 2 + [_full((1, c))],
        out_specs=[_rows(tf, c), _full((1, c))],
        out_shape=[jax.ShapeDtypeStruct((2 * L, c), BF16), jax.ShapeDtypeStruct((1, c), F32)],
        compiler_params=_cparams(("arbitrary",)), name="hyena_filter",
    )(jnp.asarray(_filter_features(L)), *_split_bf16(_block_diag2(w1)), row2x(e_filt_b1[0]), row2x(e_filt_f1[0]),
      *_split_bf16(_block_diag2(e_filt_w2[0])), row2x(e_filt_b2[0]), row2x(e_filt_f2[0]),
      *_split_bf16(w3), jnp.asarray(_decay_rates()))

    width = DFT_N2 * c
    a_data = _dft_first(f32c(dft["fd"]), z.reshape(n1, DFT_N2, c), name="dft_data")
    a_filt = _dft_first(f32c(dft["ff"]), kfilt.reshape(n1, DFT_N2, c), name="dft_filter")
    slab = pl.BlockSpec((2, SPECTRAL_SLABS, DFT_N2, c), lambda i: (0, i, 0, 0))
    twspec = pl.BlockSpec((SPECTRAL_SLABS, DFT_N2, 1), lambda i: (i, 0, 0))
    bspec = pl.pallas_call(
        _spectral_kernel,
        grid=(n1 // SPECTRAL_SLABS,),
        in_specs=[slab, slab, twspec, twspec, _full((2 * DFT_N2, 2 * DFT_N2)), _full((2 * DFT_N2, 2 * DFT_N2)),
                  _full((1, c)), _full((1, c))],
        out_specs=slab,
        out_shape=jax.ShapeDtypeStruct((2, n1, DFT_N2, c), BF16),
        compiler_params=_cparams(("parallel",)), name="hyena_spectral",
    )(a_data.reshape(2, n1, DFT_N2, c), a_filt.reshape(2, n1, DFT_N2, c),
      jnp.asarray(dft["twr"]), jnp.asarray(dft["twi"]), f32c(dft["fb"]), f32c(dft["ib"]), l1, row2(e_hy_d[0]))
    yb = _dft_last(f32c(dft["if2"]), bspec.reshape(2 * n1, width), x0g.reshape(n1, DFT_N2, c),
                   name="dft_inverse").reshape(T, c)

    x1 = pl.pallas_call(
        _even_out_kernel, grid=(T // tm,),
        in_specs=[_rows(tm, D), _rows(tm, A_WIDTH), _rows(tm, c), _full((A_WIDTH + c, D))],
        out_specs=_rows(tm, D), out_shape=jax.ShapeDtypeStruct((T, D), F32),
        compiler_params=_cparams(("parallel",)), name="even_out",
    )(xt, ya, yb, e_w_out[0].astype(BF16))

    in_idx, in_keep, q_idx, q_keep, k_idx, k_keep, v_idx, v_keep = _odd_columns()
    w_o = _gather_cols(o_w_in[0], in_idx, in_keep)
    w_q = _gather_cols(o_w_qb[0], q_idx, q_keep)
    w_k = _gather_cols(o_w_kvb[0], k_idx, k_keep)
    w_v = _gather_cols(o_w_kvb[0], v_idx, v_keep)
    cs_c, sn_c = _rope_tables_c(L)
    hq = C_HEADS * LANES
    q_c, kt_c, v_c, gc = pl.pallas_call(
        functools.partial(_odd_in_kernel, scale=(C_NOPE + C_ROPE) ** -0.5 * LOG2E),
        grid=(nt,),
        in_specs=[_rows(tm, D), _full((1, D)), _full((D, w_o.shape[1])), _full((1, C_Q_RANK)),
                  _full((1, C_KV_RANK)), _full((C_Q_RANK, hq)), _full((C_KV_RANK, hq)),
                  _full((C_KV_RANK, hq)), table, table],
        out_specs=[_rows(tm, hq), pl.BlockSpec((hq, tm), lambda i: (0, i)), _rows(tm, hq),
                   _rows(tm, C_WIDTH)],
        out_shape=[jax.ShapeDtypeStruct((T, hq), BF16), jax.ShapeDtypeStruct((hq, T), BF16),
                   jax.ShapeDtypeStruct((T, hq), BF16), jax.ShapeDtypeStruct((T, C_WIDTH), F32)],
        compiler_params=_cparams(("parallel",)), name="odd_in",
    )(x1, row2(o_norm[0]), w_o, row2(o_q_a_norm[0]), row2(o_kv_a_norm[0]), w_q, w_k, w_v,
      jnp.asarray(cs_c), jnp.asarray(sn_c))

    yc = _attention(q_c, kt_c, v_c, gc, B=B, L=L, n_pairs=C_HEADS // 2,
                    qa_col=lambda j: 2 * j, qb_col=lambda j: 2 * j + 1,
                    kta_row=lambda j: 2 * j, ktb_row=lambda j: 2 * j + 1,
                    va_col=lambda j: 2 * j, vb_col=lambda j: 2 * j + 1, tq=tq, tk=tk)

    out = pl.pallas_call(
        _odd_out_kernel, grid=(T // tm,),
        in_specs=[_rows(tm, D), _rows(tm, C_WIDTH), _full((C_WIDTH, D)), _full((1, D))],
        out_specs=_rows(tm, D), out_shape=jax.ShapeDtypeStruct((T, D), F32),
        compiler_params=_cparams(("parallel",)), name="odd_out",
    )(x1, yc, o_w_out[0].astype(BF16), row2(final_norm))
    return out.reshape(B, L, D)
```

```python
import functools
import math

import numpy as np
import jax
import jax.numpy as jnp
from jax import lax
from jax.experimental import pallas as pl
from jax.experimental.pallas import tpu as pltpu

F32 = jnp.float32
BF16 = jnp.bfloat16

D_MODEL = 1024
GRID_W = 64
EPS = 1e-6
ROPE_THETA = 10000.0

A_HEADS, A_KV_HEADS, A_HEAD_DIM = 8, 2, 64
A_WIDTH = A_HEADS * A_HEAD_DIM
HY_C = D_MODEL // 2
HY_EMB_BANDS = 16
HY_EMB_DIM = 1 + 2 * HY_EMB_BANDS
HY_HIDDEN = 64
HY_DECAY_TARGET, HY_FAST_DECAY, HY_SLOW_DECAY = 1e-2, 0.3, 1.5
EVEN_IN = 3328

C_HEADS, C_NOPE, C_ROPE, C_V = 16, 64, 32, 64
C_Q_RANK, C_KV_RANK = 384, 256
C_WIDTH = C_HEADS * C_V

LANES = 128
DFT_N2 = 128
FEAT_W = 64
FILTER_ROWS = 1024
DFT_COLS = 16
ROW_TILE = 1024
VMEM_LIMIT = 62 * 1024 * 1024
LOG2E = math.log2(math.e)
ATTN_ROW_CHUNK = 32
ATTN_TQ, ATTN_TK = 512, 2048
ATTN_SUBTILES = 1
SPECTRAL_SLABS = 8


def _cparams(sem):
    return pltpu.CompilerParams(dimension_semantics=sem, vmem_limit_bytes=VMEM_LIMIT)


def _silu(x):
    return x * (1.0 / (1.0 + jnp.exp(-x)))


def _dot(a, b):
    return jnp.dot(a, b, preferred_element_type=F32)


def _split_bf16(a):
    hi = a.astype(BF16)
    return hi, (a - hi.astype(F32)).astype(BF16)


def _dot3(a, w_hi, w_lo):
    a_hi, a_lo = _split_bf16(a)
    return _dot(a_hi, w_hi) + (_dot(a_hi, w_lo) + _dot(a_lo, w_hi))


def _axial_angles(L, dim):
    rows = L // GRID_W
    row = np.repeat(np.arange(rows), GRID_W).astype(np.float64)
    col = np.tile(np.arange(GRID_W), rows).astype(np.float64)
    n = dim // 4
    inv = ROPE_THETA ** (-np.arange(n, dtype=np.float64) / n)
    return np.concatenate([row[:, None] * inv, col[:, None] * inv], axis=-1)


def _rope_tables_a(L):
    ang = _axial_angles(L, A_HEAD_DIM)
    lane = np.arange(LANES)
    j = lane % 32
    sign = np.where(lane // 64 == 0, -1.0, 1.0)
    return (np.cos(ang)[:, j].astype(np.float32), (np.sin(ang)[:, j] * sign).astype(np.float32))


def _rope_tables_c(L):
    ang = _axial_angles(L, C_ROPE)
    cs = np.zeros((L, LANES)); sn = np.zeros((L, LANES))
    cs[:, 0:32] = 1.0; cs[:, 64:96] = 1.0
    cs[:, 32:48] = np.cos(ang); cs[:, 96:112] = np.cos(ang)
    sn[:, 32:48] = -np.sin(ang); sn[:, 96:112] = np.sin(ang)
    return cs.astype(np.float32), sn.astype(np.float32)


def _even_columns():
    q0, k0, v0, rest0 = 0, 512, 640, 768
    idx, keep = [], []
    lane = np.arange(LANES)
    part, hs, j = lane // 64, (lane % 64) // 32, lane % 32
    for pair in range(A_HEADS // 2):
        idx.append(q0 + (2 * pair + hs) * 64 + 32 * part + j); keep.append(np.ones(LANES))
    for g in range(A_KV_HEADS):
        for var in range(2):
            idx.append(k0 + g * 64 + 32 * part + j); keep.append((hs == var).astype(np.float64))
    for g in range(A_KV_HEADS):
        idx.append(v0 + g * 64 + lane % 64); keep.append((lane < 64).astype(np.float64))
    idx.append(np.arange(rest0, EVEN_IN)); keep.append(np.ones(EVEN_IN - rest0))
    return np.concatenate(idx).astype(np.int32), np.concatenate(keep).astype(np.float32)


def _head_gain_index():
    lane = np.arange(LANES)
    return (32 * (lane // 64) + lane % 32).astype(np.int32)


def _segment_mean_matrix():
    lane = np.arange(LANES)
    cls = (lane % 64) // 32
    return ((cls[:, None] == cls[None, :]) / float(A_HEAD_DIM)).astype(np.float32)


def _mla_head_lane_sources():
    src = -np.ones(LANES, np.int64)
    src[0:32] = np.arange(32)
    src[32:48] = 64 + np.arange(16)
    src[64:96] = 32 + np.arange(32)
    src[96:112] = 80 + np.arange(16)
    return src


def _odd_columns():
    src = _mla_head_lane_sources()
    kr = np.where(src >= 64, 640 + (src - 64), 0)
    kr_keep = (src >= 64).astype(np.float32)
    in_idx = np.concatenate([np.arange(0, 640), kr, np.arange(672, 1696)]).astype(np.int32)
    in_keep = np.concatenate([np.ones(640), kr_keep, np.ones(1024)]).astype(np.float32)
    q_idx, q_keep, k_idx, k_keep, v_idx, v_keep = [], [], [], [], [], []
    lane = np.arange(LANES)
    for h in range(C_HEADS):
        q_idx.append(np.where(src >= 0, h * (C_NOPE + C_ROPE) + src, 0)); q_keep.append(src >= 0)
        nope = (src >= 0) & (src < 64)
        k_idx.append(np.where(nope, h * (C_NOPE + C_V) + src, 0)); k_keep.append(nope)
        v_idx.append(h * (C_NOPE + C_V) + C_NOPE + lane % C_V); v_keep.append(lane < C_V)
    cat = lambda xs, dt: np.concatenate(xs).astype(dt)
    return (in_idx, in_keep, cat(q_idx, np.int32), cat(q_keep, np.float32),
            cat(k_idx, np.int32), cat(k_keep, np.float32), cat(v_idx, np.int32), cat(v_keep, np.float32))


def _gather_cols(w, idx, keep):
    w = w.astype(BF16)
    pieces, start, n = [], 0, len(idx)
    while start < n:
        end = start + 1
        while end < n and keep[end] == keep[start] and (keep[start] == 0 or idx[end] == idx[end - 1] + 1):
            end += 1
        if keep[start]:
            pieces.append(w[:, int(idx[start]):int(idx[start]) + end - start])
        else:
            pieces.append(jnp.zeros((w.shape[0], end - start), w.dtype))
        start = end
    return jnp.concatenate(pieces, axis=1)


def _dft_constants(L):
    n1 = 2 * L // DFT_N2
    n = n1 * DFT_N2
    h = n1 // 2
    a1 = -2.0 * np.pi * np.outer(np.arange(n1), np.arange(n1)) / n1
    f1r, f1i = np.cos(a1), np.sin(a1)
    a2 = -2.0 * np.pi * np.outer(np.arange(DFT_N2), np.arange(DFT_N2)) / DFT_N2
    f2r, f2i = np.cos(a2), np.sin(a2)
    at = -2.0 * np.pi * np.outer(np.arange(n1), np.arange(DFT_N2)) / n
    fd = np.block([[f1r[:, :h], -f1i[:, :h]], [f1i[:, :h], f1r[:, :h]]])
    ff = np.concatenate([f1r, f1i], axis=0)
    fb = np.block([[f2r, -f2i], [f2i, f2r]])
    ib = np.block([[f2r, f2i], [-f2i, f2r]])
    if2 = np.block([[f1r[:h], f1i[:h]], [-f1i[:h], f1r[:h]]]) / n
    return dict(n1=n1, fd=fd, ff=ff, fb=fb, ib=ib, if2=if2,
                twr=np.cos(at)[:, :, None].astype(np.float32),
                twi=np.sin(at)[:, :, None].astype(np.float32))


def _filter_features(L):
    r = np.arange(2 * L)
    lag = np.where(r < L, r, 2 * L - r)
    lag = np.where(r == L, 0, lag)
    t = lag / float(L - 1)
    w = 2.0 * np.pi * lag / float(L)
    bands = np.linspace(1e-4, HY_EMB_BANDS - 1, HY_EMB_BANDS)
    feat = np.zeros((2 * L, FEAT_W))
    feat[:, 0] = t
    feat[:, 1:1 + HY_EMB_BANDS] = np.cos(bands[None] * w[:, None])
    feat[:, 1 + HY_EMB_BANDS:HY_EMB_DIM] = -np.sin(bands[None] * w[:, None])
    feat[:, HY_EMB_DIM] = (r != L)
    half = FILTER_ROWS // 2
    tiles = feat.reshape(-1, 2, half, FEAT_W)
    return np.concatenate([tiles[:, 0], tiles[:, 1]], axis=-1).reshape(-1, 2 * FEAT_W).astype(np.float32)


def _block_diag2(w):
    z = jnp.zeros_like(w)
    return jnp.concatenate([jnp.concatenate([w, z], axis=1), jnp.concatenate([z, w], axis=1)], axis=0)


def _decay_rates():
    lo = math.log(HY_DECAY_TARGET) / HY_SLOW_DECAY
    hi = math.log(HY_DECAY_TARGET) / HY_FAST_DECAY
    return np.abs(np.linspace(lo, hi, HY_C)).astype(np.float32)[None]


def _rms_rows(x, gain):
    return x * lax.rsqrt(jnp.mean(x * x, axis=-1, keepdims=True) + EPS) * gain


def _rope(y, cs, sn):
    return y * cs + pltpu.roll(y, 64, 1) * sn


def _with_ones_half(v):
    lane = lax.broadcasted_iota(jnp.int32, v.shape, 1)
    return jnp.where(lane % LANES < 64, v, 1.0)


def _even_in_kernel(x_ref, xprev_ref, xnext_ref, g_ref, w_ref, m_ref, gq_ref, gk_ref, cs_ref, sn_ref,
                    cw_ref, cb_ref, q_ref, kt_ref, v_ref, ga_ref, z_ref, x0g_ref, *, scale, tiles_per_seq):
    tm = x_ref.shape[0]
    x_all = jnp.concatenate([x_ref[...], xprev_ref[...], xnext_ref[...]], axis=0)
    h_all = _rms_rows(x_all, g_ref[...]).astype(BF16)
    h = h_all[0:tm]
    cs, sn, m = cs_ref[...], sn_ref[...], m_ref[...]

    def norm_rope(p, gain):
        ms = _dot((p * p).astype(BF16), m)
        return _rope(p * lax.rsqrt(ms + EPS) * gain, cs, sn)

    pq = _dot(h, w_ref[:, 0:512])
    pk = _dot(h, w_ref[:, 512:1024])
    for b in range(4):
        sl = slice(b * LANES, (b + 1) * LANES)
        q_ref[:, sl] = (norm_rope(pq[:, sl], gq_ref[...]) * scale).astype(BF16)
        kt_ref[sl, :] = norm_rope(pk[:, sl], gk_ref[...]).T.astype(BF16)
    v_ref[...] = _with_ones_half(_dot(h, w_ref[:, 1024:1280])).astype(BF16)
    ga_ref[...] = _silu(_dot(h, w_ref[:, 1280:1792]))
    gb = _silu(_dot(h, w_ref[:, 3328:3840]))

    hy_all = _dot(h_all, w_ref[:, 1792:3328])
    p = hy_all[0:tm]
    i = pl.program_id(0)
    first = (i % tiles_per_seq) == 0
    last = (i % tiles_per_seq) == tiles_per_seq - 1
    halo_prev = jnp.where(first, 0.0, hy_all[tm + 7:tm + 8])
    halo_next = jnp.where(last, 0.0, hy_all[tm + 8:tm + 9])
    row = lax.broadcasted_iota(jnp.int32, p.shape, 0)
    p_prev = jnp.where(row == 0, halo_prev, pltpu.roll(p, 1, 0))
    p_next = jnp.where(row == tm - 1, halo_next, pltpu.roll(p, tm - 1, 0))
    u = p_prev * cw_ref[0:1, :] + p * cw_ref[1:2, :] + p_next * cw_ref[2:3, :] + cb_ref[...]
    c = HY_C
    z_ref[...] = (u[:, 2 * c:3 * c] * u[:, c:2 * c]).astype(BF16)
    x0g_ref[...] = u[:, 0:c] * gb


def _odd_in_kernel(x_ref, g_ref, w_ref, gq_ref, gkv_ref, wq_ref, wk_ref, wv_ref, cs_ref, sn_ref,
                   q_ref, kt_ref, v_ref, gc_ref, *, scale):
    h = _rms_rows(x_ref[...], g_ref[...]).astype(BF16)
    cs, sn = cs_ref[...], sn_ref[...]
    cq = _rms_rows(_dot(h, w_ref[:, 0:384]), gq_ref[...]).astype(BF16)
    ckv = _rms_rows(_dot(h, w_ref[:, 384:640]), gkv_ref[...]).astype(BF16)
    kr = _rope(_dot(h, w_ref[:, 640:768]), cs, sn)
    gc_ref[...] = _silu(_dot(h, w_ref[:, 768:1792]))
    q = _dot(cq, wq_ref[...])
    kn = _dot(ckv, wk_ref[...])
    for b in range(C_HEADS):
        sl = slice(b * LANES, (b + 1) * LANES)
        q_ref[:, sl] = (_rope(q[:, sl], cs, sn) * scale).astype(BF16)
        kt_ref[sl, :] = (kn[:, sl] + kr).T.astype(BF16)
    v_ref[...] = _with_ones_half(_dot(ckv, wv_ref[...])).astype(BF16)


def _attn_kernel(qa_ref, qb_ref, kta_ref, ktb_ref, va_ref, vb_ref, g_ref, o_ref, *scratch, tq, tk):
    nsub = qa_ref.shape[0] // tq
    nk = va_ref.shape[0] // tk
    nb = tk // LANES
    assert nk % 2 == 0
    grid22 = lambda refs: ((refs[0], refs[1]), (refs[2], refs[3]))
    s_scr, p_scr = grid22(scratch[0:4]), grid22(scratch[4:8])
    m_all = (grid22(scratch[8:12]), grid22(scratch[12:16]))
    a_all = (grid22(scratch[16:20]), grid22(scratch[20:24]))
    acc_all = (scratch[24:26], scratch[26:28])
    q_refs = (qa_ref, qb_ref)
    kts = (kta_ref, ktb_ref)
    vs = (va_ref, vb_ref)

    def init(sub):
        for h in range(2):
            m_all[sub % 2][1][h][...] = jnp.full((tq, LANES), -jnp.inf, F32)
            acc_all[sub % 2][h][...] = jnp.zeros((tq, LANES), F32)

    def score_tile(sub, t, c):
        m_scr, a_scr = m_all[sub % 2], a_all[sub % 2]
        off = pl.multiple_of(t * tk, tk)
        for h in range(2):
            s = _dot(q_refs[h][sub * tq:(sub + 1) * tq, :], kts[h][:, pl.ds(off, tk)])
            s_scr[c][h][...] = s
            tile_max = functools.reduce(jnp.maximum, [s[:, j * LANES:(j + 1) * LANES] for j in range(nb)])
            m_prev = m_scr[1 - c][h][...]
            m_new = jnp.maximum(m_prev, jnp.max(tile_max, axis=-1, keepdims=True))
            m_scr[c][h][...] = m_new
            a_scr[c][h][...] = jnp.exp2(m_prev - m_new)

    def consume(sub, t, c):
        m_scr, a_scr, acc_scr = m_all[sub % 2], a_all[sub % 2], acc_all[sub % 2]
        off = pl.multiple_of(t * tk, tk)
        for h in range(2):
            for r in range(0, tq, ATTN_ROW_CHUNK):
                rows = slice(r, r + ATTN_ROW_CHUNK)
                m_new = m_scr[c][h][rows]
                for j in range(nb):
                    cols = slice(j * LANES, (j + 1) * LANES)
                    p_scr[c][h][rows, cols] = jnp.exp2(s_scr[c][h][rows, cols] - m_new).astype(BF16)
            pv = _dot(p_scr[c][h][...], vs[h][pl.ds(off, tk), :])
            acc_scr[h][...] = a_scr[c][h][...] * acc_scr[h][...] + pv

    def finalize(sub):
        rows = slice(sub * tq, (sub + 1) * tq)
        acc = [acc_all[sub % 2][h][...] for h in range(2)]
        out = [a / pltpu.roll(a, 64, 1) for a in acc]
        lane = lax.broadcasted_iota(jnp.int32, (tq, LANES), 1)
        o_ref[rows, :] = (jnp.where(lane < 64, out[0], pltpu.roll(out[1], 64, 1)) * g_ref[rows, :]).astype(o_ref.dtype)

    init(0)
    score_tile(0, 0, 0)
    for sub in range(nsub):

        def body(j, carry, sub=sub):
            t = 2 * j
            score_tile(sub, t + 1, 1)
            consume(sub, t, 0)
            score_tile(sub, t + 2, 0)
            consume(sub, t + 1, 1)
            return carry

        lax.fori_loop(0, nk // 2 - 1, body, 0)
        score_tile(sub, nk - 1, 1)
        consume(sub, nk - 2, 0)
        if sub + 1 < nsub:
            init(sub + 1)
            score_tile(sub + 1, 0, 0)
        consume(sub, nk - 1, 1)
        finalize(sub)


def _filter_kernel(feat_ref, w1h_ref, w1l_ref, b1_ref, f1_ref, w2h_ref, w2l_ref, b2_ref, f2_ref,
                   w3h_ref, w3l_ref, rate_ref, k_ref, l1_ref):
    feat = feat_ref[...]
    half, c = feat.shape[0], rate_ref.shape[1]
    h = jnp.sin(f1_ref[...] * (_dot3(feat, w1h_ref[...], w1l_ref[...]) + b1_ref[...]))
    h = jnp.sin(f2_ref[...] * (_dot3(h, w2h_ref[...], w2l_ref[...]) + b2_ref[...]))
    k2 = _dot3(h, w3h_ref[...], w3l_ref[...])

    @pl.when(pl.program_id(0) == 0)
    def _():
        l1_ref[...] = jnp.zeros_like(l1_ref)

    for part in range(2):
        t = feat[:, part * FEAT_W:part * FEAT_W + 1]
        keep = feat[:, part * FEAT_W + HY_EMB_DIM:part * FEAT_W + HY_EMB_DIM + 1]
        k = k2[:, part * c:(part + 1) * c] * jnp.exp(-t * rate_ref[...]) * keep
        k_ref[part * half:(part + 1) * half, :] = k.astype(BF16)
        l1_ref[...] += jnp.sum(jnp.abs(k), axis=0, keepdims=True)


def _dft_first_kernel(m_ref, x_ref, o_ref):
    m = m_ref[...].astype(BF16)
    c = x_ref.shape[2]
    for j in range(x_ref.shape[1]):
        o_ref[:, j * c:(j + 1) * c] = _dot(m, x_ref[:, j, :]).astype(o_ref.dtype)


def _dft_last_kernel(m_ref, x_ref, g_ref, o_ref):
    m = m_ref[...].astype(BF16)
    c = g_ref.shape[2]
    for j in range(g_ref.shape[1]):
        o_ref[:, j, :] = (_dot(m, x_ref[:, j * c:(j + 1) * c]) * g_ref[:, j, :]).astype(o_ref.dtype)


def _spectral_kernel(ad_ref, af_ref, twr_ref, twi_ref, fb_ref, ib_ref, l1_ref, d_ref, o_ref):
    fb = fb_ref[...].astype(BF16)
    ib = ib_ref[...].astype(BF16)
    inv_l1 = 1.0 / l1_ref[...]
    for s in range(ad_ref.shape[1]):
        twr, twi = twr_ref[s], twi_ref[s]

        def stage2(a_ref):
            ar, ai = a_ref[0, s].astype(F32), a_ref[1, s].astype(F32)
            x = jnp.concatenate([ar * twr - ai * twi, ar * twi + ai * twr], axis=0).astype(BF16)
            xh = _dot(fb, x)
            return xh[:DFT_N2], xh[DFT_N2:]

        xr, xi = stage2(ad_ref)
        kr, ki = stage2(af_ref)
        kr = kr * inv_l1 + d_ref[...]
        ki = ki * inv_l1
        y = jnp.concatenate([xr * kr - xi * ki, xr * ki + xi * kr], axis=0).astype(BF16)
        bt = _dot(ib, y)
        br, bi = bt[:DFT_N2], bt[DFT_N2:]
        o_ref[0, s] = (br * twr + bi * twi).astype(o_ref.dtype)
        o_ref[1, s] = (bi * twr - br * twi).astype(o_ref.dtype)


def _even_out_kernel(x_ref, ya_ref, yb_ref, w_ref, o_ref):
    o_ref[...] = x_ref[...] + _dot(ya_ref[...], w_ref[0:A_WIDTH, :]) + _dot(yb_ref[...], w_ref[A_WIDTH:, :])


def _odd_out_kernel(x_ref, y_ref, w_ref, g_ref, o_ref):
    o_ref[...] = _rms_rows(x_ref[...] + _dot(y_ref[...], w_ref[...]), g_ref[...])


def _full(shape):
    return pl.BlockSpec(shape, lambda *_: (0,) * len(shape))


def _rows(tm, width):
    return pl.BlockSpec((tm, width), lambda i: (i, 0))


def _attention(q, kt, v, gate, *, B, L, n_pairs, qa_col, qb_col, kta_row, ktb_row, va_col, vb_col, tq, tk):
    T = B * L
    rows = tq * ATTN_SUBTILES
    nq = L // rows
    qspec = lambda col: pl.BlockSpec((rows, LANES), lambda b, j, i: (b * nq + i, col(j)))
    ktspec = lambda row: pl.BlockSpec((LANES, L), lambda b, j, i: (row(j), b))
    vspec = lambda col: pl.BlockSpec((L, LANES), lambda b, j, i: (b, col(j)))
    return pl.pallas_call(
        functools.partial(_attn_kernel, tq=tq, tk=tk),
        grid=(B, n_pairs, nq),
        in_specs=[qspec(qa_col), qspec(qb_col), ktspec(kta_row), ktspec(ktb_row),
                  vspec(va_col), vspec(vb_col), qspec(lambda j: j)],
        out_specs=qspec(lambda j: j),
        out_shape=jax.ShapeDtypeStruct((T, n_pairs * LANES), BF16),
        scratch_shapes=([pltpu.VMEM((tq, tk), F32)] * 4 + [pltpu.VMEM((tq, tk), BF16)] * 4
                        + [pltpu.VMEM((tq, LANES), F32)] * 20),
        compiler_params=_cparams(("parallel", "parallel", "arbitrary")),
        name="attention",
    )(q, q, kt, kt, v, v, gate)


def _dft_first(mat, x3, *, name):
    m, k = mat.shape
    _, n2, c = x3.shape
    w = DFT_COLS
    return pl.pallas_call(
        _dft_first_kernel, grid=(n2 // w,),
        in_specs=[_full((m, k)), pl.BlockSpec((k, w, c), lambda i: (0, i, 0))],
        out_specs=pl.BlockSpec((m, w * c), lambda i: (0, i)),
        out_shape=jax.ShapeDtypeStruct((m, n2 * c), BF16),
        compiler_params=_cparams(("parallel",)), name=name,
    )(mat, x3)


def _dft_last(mat, x, gate3, *, name):
    m, k = mat.shape
    _, n2, c = gate3.shape
    w = DFT_COLS
    tspec = pl.BlockSpec((m, w, c), lambda i: (0, i, 0))
    return pl.pallas_call(
        _dft_last_kernel, grid=(n2 // w,),
        in_specs=[_full((m, k)), pl.BlockSpec((k, w * c), lambda i: (0, i)), tspec],
        out_specs=tspec,
        out_shape=jax.ShapeDtypeStruct((m, n2, c), BF16),
        compiler_params=_cparams(("parallel",)), name=name,
    )(mat, x, gate3)


def kernel(x, e_norm, e_w_in, e_q_norm, e_k_norm, e_conv_w, e_conv_b, e_filt_w1, e_filt_b1, e_filt_f1,
           e_filt_w2, e_filt_b2, e_filt_f2, e_filt_w3, e_hy_d, e_w_out,
           o_norm, o_w_in, o_q_a_norm, o_w_qb, o_kv_a_norm, o_w_kvb, o_w_out, final_norm):
    B, L, D = x.shape
    assert D == D_MODEL and B == 2 and L % 1024 == 0 and L % GRID_W == 0
    T = B * L
    tm = min(ROW_TILE, L)
    tq = min(ATTN_TQ, L)
    tk = min(ATTN_TK, L // 2)
    nt = T // tm
    tiles_per_seq = L // tm
    xt = x.reshape(T, D)
    row2 = lambda a: a.reshape(1, -1).astype(F32)

    idx, keep = _even_columns()
    w_e = _gather_cols(e_w_in[0], idx, keep)
    ne = w_e.shape[1]
    cs_a, sn_a = _rope_tables_a(L)
    gidx = _head_gain_index()
    table = pl.BlockSpec((tm, LANES), lambda i: (i % tiles_per_seq, 0))
    c = HY_C
    sub = tm // 8
    q_a, kt_a, v_a, ga, z, x0g = pl.pallas_call(
        functools.partial(_even_in_kernel, scale=A_HEAD_DIM ** -0.5 * LOG2E, tiles_per_seq=tiles_per_seq),
        grid=(nt,),
        in_specs=[_rows(tm, D),
                  pl.BlockSpec((8, D), lambda i: (jnp.maximum(i * sub - 1, 0), 0)),
                  pl.BlockSpec((8, D), lambda i: (jnp.minimum((i + 1) * sub, T // 8 - 1), 0)),
                  _full((1, D)), _full((D, ne)), _full((LANES, LANES)),
                  _full((1, LANES)), _full((1, LANES)), table, table, _full((3, 3 * c)), _full((1, 3 * c))],
        out_specs=[_rows(tm, 512), pl.BlockSpec((512, tm), lambda i: (0, i)), _rows(tm, 256),
                   _rows(tm, 512), _rows(tm, c), _rows(tm, c)],
        out_shape=[jax.ShapeDtypeStruct((T, 512), BF16), jax.ShapeDtypeStruct((512, T), BF16),
                   jax.ShapeDtypeStruct((T, 256), BF16), jax.ShapeDtypeStruct((T, 512), F32),
                   jax.ShapeDtypeStruct((T, c), BF16), jax.ShapeDtypeStruct((T, c), F32)],
        compiler_params=_cparams(("parallel",)), name="even_in",
    )(xt, xt, xt, row2(e_norm[0]), w_e, jnp.asarray(_segment_mean_matrix(), BF16),
      row2(e_q_norm[0][gidx]), row2(e_k_norm[0][gidx]), jnp.asarray(cs_a), jnp.asarray(sn_a),
      e_conv_w[0].astype(F32), row2(e_conv_b[0]))

    ya = _attention(q_a, kt_a, v_a, ga, B=B, L=L, n_pairs=A_HEADS // 2,
                    qa_col=lambda j: j, qb_col=lambda j: j,
                    kta_row=lambda j: 2 * (j // 2), ktb_row=lambda j: 2 * (j // 2) + 1,
                    va_col=lambda j: j // 2, vb_col=lambda j: j // 2, tq=tq, tk=tk)

    dft = _dft_constants(L)
    n1 = dft["n1"]
    f32c = lambda a: jnp.asarray(a, F32)
    w1 = jnp.zeros((FEAT_W, HY_HIDDEN), F32).at[:HY_EMB_DIM].set(e_filt_w1[0])
    tf, hid2 = FILTER_ROWS, 2 * HY_HIDDEN
    row2x = lambda a: jnp.tile(row2(a), (1, 2))
    w3_sides = [_block_diag2(e_filt_w3[0][:, side * c:(side + 1) * c]) for side in range(2)]
    w3 = jnp.stack(w3_sides)
    w3spec = pl.BlockSpec((None, hid2, 2 * c), lambda i: (i // (L // tf), 0, 0))
    kfilt, l1 = pl.pallas_call(
        _filter_kernel,
        grid=(2 * L // tf,),
        in_specs=[_rows(tf // 2, 2 * FEAT_W)] + [_full((2 * FEAT_W, hid2))] * 2 + [_full((1, hid2))] * 2
                 + [_full((hid2, hid2))] * 2 + [_full((1, hid2))] * 2 + [w3spec] * 2 + [_full((1, c))],
        out_specs=[_rows(tf, c), _full((1, c))],
        out_shape=[jax.ShapeDtypeStruct((2 * L, c), BF16), jax.ShapeDtypeStruct((1, c), F32)],
        compiler_params=_cparams(("arbitrary",)), name="hyena_filter",
    )(jnp.asarray(_filter_features(L)), *_split_bf16(_block_diag2(w1)), row2x(e_filt_b1[0]), row2x(e_filt_f1[0]),
      *_split_bf16(_block_diag2(e_filt_w2[0])), row2x(e_filt_b2[0]), row2x(e_filt_f2[0]),
      *_split_bf16(w3), jnp.asarray(_decay_rates()))

    width = DFT_N2 * c
    a_data = _dft_first(f32c(dft["fd"]), z.reshape(n1, DFT_N2, c), name="dft_data")
    a_filt = _dft_first(f32c(dft["ff"]), kfilt.reshape(n1, DFT_N2, c), name="dft_filter")
    slab = pl.BlockSpec((2, SPECTRAL_SLABS, DFT_N2, c), lambda i: (0, i, 0, 0))
    twspec = pl.BlockSpec((SPECTRAL_SLABS, DFT_N2, 1), lambda i: (i, 0, 0))
    bspec = pl.pallas_call(
        _spectral_kernel,
        grid=(n1 // SPECTRAL_SLABS,),
        in_specs=[slab, slab, twspec, twspec, _full((2 * DFT_N2, 2 * DFT_N2)), _full((2 * DFT_N2, 2 * DFT_N2)),
                  _full((1, c)), _full((1, c))],
        out_specs=slab,
        out_shape=jax.ShapeDtypeStruct((2, n1, DFT_N2, c), BF16),
        compiler_params=_cparams(("parallel",)), name="hyena_spectral",
    )(a_data.reshape(2, n1, DFT_N2, c), a_filt.reshape(2, n1, DFT_N2, c),
      jnp.asarray(dft["twr"]), jnp.asarray(dft["twi"]), f32c(dft["fb"]), f32c(dft["ib"]), l1, row2(e_hy_d[0]))
    yb = _dft_last(f32c(dft["if2"]), bspec.reshape(2 * n1, width), x0g.reshape(n1, DFT_N2, c),
                   name="dft_inverse").reshape(T, c)

    x1 = pl.pallas_call(
        _even_out_kernel, grid=(T // tm,),
        in_specs=[_rows(tm, D), _rows(tm, A_WIDTH), _rows(tm, c), _full((A_WIDTH + c, D))],
        out_specs=_rows(tm, D), out_shape=jax.ShapeDtypeStruct((T, D), F32),
        compiler_params=_cparams(("parallel",)), name="even_out",
    )(xt, ya, yb, e_w_out[0].astype(BF16))

    in_idx, in_keep, q_idx, q_keep, k_idx, k_keep, v_idx, v_keep = _odd_columns()
    w_o = _gather_cols(o_w_in[0], in_idx, in_keep)
    w_q = _gather_cols(o_w_qb[0], q_idx, q_keep)
    w_k = _gather_cols(o_w_kvb[0], k_idx, k_keep)
    w_v = _gather_cols(o_w_kvb[0], v_idx, v_keep)
    cs_c, sn_c = _rope_tables_c(L)
    hq = C_HEADS * LANES
    q_c, kt_c, v_c, gc = pl.pallas_call(
        functools.partial(_odd_in_kernel, scale=(C_NOPE + C_ROPE) ** -0.5 * LOG2E),
        grid=(nt,),
        in_specs=[_rows(tm, D), _full((1, D)), _full((D, w_o.shape[1])), _full((1, C_Q_RANK)),
                  _full((1, C_KV_RANK)), _full((C_Q_RANK, hq)), _full((C_KV_RANK, hq)),
                  _full((C_KV_RANK, hq)), table, table],
        out_specs=[_rows(tm, hq), pl.BlockSpec((hq, tm), lambda i: (0, i)), _rows(tm, hq),
                   _rows(tm, C_WIDTH)],
        out_shape=[jax.ShapeDtypeStruct((T, hq), BF16), jax.ShapeDtypeStruct((hq, T), BF16),
                   jax.ShapeDtypeStruct((T, hq), BF16), jax.ShapeDtypeStruct((T, C_WIDTH), F32)],
        compiler_params=_cparams(("parallel",)), name="odd_in",
    )(x1, row2(o_norm[0]), w_o, row2(o_q_a_norm[0]), row2(o_kv_a_norm[0]), w_q, w_k, w_v,
      jnp.asarray(cs_c), jnp.asarray(sn_c))

    yc = _attention(q_c, kt_c, v_c, gc, B=B, L=L, n_pairs=C_HEADS // 2,
                    qa_col=lambda j: 2 * j, qb_col=lambda j: 2 * j + 1,
                    kta_row=lambda j: 2 * j, ktb_row=lambda j: 2 * j + 1,
                    va_col=lambda j: 2 * j, vb_col=lambda j: 2 * j + 1, tq=tq, tk=tk)

    out = pl.pallas_call(
        _odd_out_kernel, grid=(T // tm,),
        in_specs=[_rows(tm, D), _rows(tm, C_WIDTH), _full((C_WIDTH, D)), _full((1, D))],
        out_specs=_rows(tm, D), out_shape=jax.ShapeDtypeStruct((T, D), F32),
        compiler_params=_cparams(("parallel",)), name="odd_out",
    )(x1, yc, o_w_out[0].astype(BF16), row2(final_norm))
    return out.reshape(B, L, D)
```
